```python
import functools
import jax, jax.numpy as jnp
from jax import lax
import numpy as np


D_MODEL = 4096
BATCH = 2
SEQ = 4096
DEPTH = 4
DEC_BATCH = 16
DEC_SEQ = 64
PAST_LEN = 1024

CHUNK = 64
HEAD_DIM = 64
SCALE = HEAD_DIM ** -0.5
HEADS_A = D_MODEL // (2 * HEAD_DIM)
BAND_CHUNKS_A = 8
BAND_ROWS_A = BAND_CHUNKS_A * CHUNK
REL_CLIP = 128
HEADS_B = D_MODEL // (2 * HEAD_DIM)
KV_HEADS_B = HEADS_B // 4
GROUP_B = HEADS_B // KV_HEADS_B
WINDOW_B = 128
BAND_CHUNKS_B = WINDOW_B // CHUNK
WIDTH_A = HEADS_A * HEAD_DIM
WIDTH_B = HEADS_B * HEAD_DIM
KV_WIDTH_B = KV_HEADS_B * HEAD_DIM
MIX_WIDTH = WIDTH_A + WIDTH_B
IN_WIDTH = 3 * WIDTH_A + WIDTH_B + 2 * KV_WIDTH_B
SPLIT_POINTS = (WIDTH_A, 2 * WIDTH_A, 3 * WIDTH_A, 3 * WIDTH_A + WIDTH_B,
                3 * WIDTH_A + WIDTH_B + KV_WIDTH_B)
N_EXPERTS = 32
TOP_K = 4
D_FF = 1536
SWIGLU_ALPHA = 1.702
SWIGLU_LIMIT = 7.0
EXPERT_BLOCK = 128
N_MOD = 6
RMS_EPS = 1e-5

kernel_name = 'hybrid_chunk_stream_encoder_step'


def _rms(x, g):
    xf = x.astype(jnp.float32)
    y = xf * lax.rsqrt(jnp.mean(xf * xf, axis=-1, keepdims=True) + RMS_EPS)
    return (y * g.astype(jnp.float32)).astype(x.dtype)


def _chunk_mask(q_pos, k_pos, n_back):
    qc = (q_pos // CHUNK)[:, None]
    kc = (k_pos // CHUNK)[None, :]
    return (k_pos[None, :] >= 0) & (kc <= qc) & (kc >= qc - n_back)


def _attn_relbias(q, k, v, q_pos, k_pos, rel_bias):
    s = jnp.einsum('bqhd,bkhd->bhqk', q, k).astype(jnp.float32) * SCALE
    dist = jnp.clip(k_pos[None, :] - q_pos[:, None], -REL_CLIP, REL_CLIP) + REL_CLIP
    s = s + jnp.take(rel_bias, dist, axis=1).astype(jnp.float32)[None]
    s = jnp.where(_chunk_mask(q_pos, k_pos, BAND_CHUNKS_A)[None, None], s, -jnp.inf)
    p = jax.nn.softmax(s, axis=-1).astype(v.dtype)
    return jnp.einsum('bhqk,bkhd->bqhd', p, v)


def _attn_sink_alibi(q, k, v, q_pos, k_pos, sinks):
    b, tq = q.shape[0], q.shape[1]
    qg = q.reshape(b, tq, KV_HEADS_B, GROUP_B, HEAD_DIM)
    s = jnp.einsum('bqhgd,bkhd->bhgqk', qg, k).astype(jnp.float32) * SCALE
    slopes = (2.0 ** (-8.0 * jnp.arange(1, HEADS_B + 1, dtype=jnp.float32) / HEADS_B)
              ).reshape(KV_HEADS_B, GROUP_B)
    dist = jnp.abs(q_pos[:, None] - k_pos[None, :]).astype(jnp.float32)
    s = s - slopes[:, :, None, None] * dist
    s = jnp.where(_chunk_mask(q_pos, k_pos, BAND_CHUNKS_B)[None, None, None], s, -jnp.inf)
    sink = jnp.broadcast_to(sinks.astype(jnp.float32).reshape(1, KV_HEADS_B, GROUP_B, 1, 1),
                            s.shape[:-1] + (1,))
    p = jax.nn.softmax(jnp.concatenate([s, sink], axis=-1), axis=-1)[..., :-1].astype(v.dtype)
    o = jnp.einsum('bhgqk,bkhd->bqhgd', p, v)
    return o.reshape(b, tq, HEADS_B, HEAD_DIM)


def _band_sweep(core, q, k, v, n_back):
    b, s = q.shape[0], q.shape[1]
    n_chunks = s // CHUNK
    band = (n_back + 1) * CHUNK
    pad = ((0, 0), (n_back * CHUNK, 0), (0, 0), (0, 0))
    kp = jnp.pad(k, pad)
    vp = jnp.pad(v, pad)
    qc = q.reshape(b, n_chunks, CHUNK, q.shape[2], HEAD_DIM).swapaxes(0, 1)

    def one_chunk(args):
        i, qi = args
        start = i * CHUNK
        ki = lax.dynamic_slice_in_dim(kp, start, band, axis=1)
        vi = lax.dynamic_slice_in_dim(vp, start, band, axis=1)
        q_pos = start + jnp.arange(CHUNK, dtype=jnp.int32)
        k_pos = start - n_back * CHUNK + jnp.arange(band, dtype=jnp.int32)
        return core(qi, ki, vi, q_pos, k_pos)

    o = lax.map(one_chunk, (jnp.arange(n_chunks, dtype=jnp.int32), qc))
    return o.swapaxes(0, 1).reshape(b, s, -1)


def _with_cache(core, q, k_new, v_new, k_cache, v_cache):
    b, t = q.shape[0], q.shape[1]
    l = k_cache.shape[1]
    k = jnp.concatenate([k_cache.astype(k_new.dtype), k_new], axis=1)
    v = jnp.concatenate([v_cache.astype(v_new.dtype), v_new], axis=1)
    q_pos = PAST_LEN + jnp.arange(t, dtype=jnp.int32)
    k_pos = jnp.concatenate([PAST_LEN - l + jnp.arange(l, dtype=jnp.int32), q_pos])
    return core(q, k, v, q_pos, k_pos).reshape(b, t, -1)


def _attend_prompt(qa, ka, va, qb, kb, vb, core_a, core_b):
    return (_band_sweep(core_a, qa, ka, va, BAND_CHUNKS_A),
            _band_sweep(core_b, qb, kb, vb, BAND_CHUNKS_B))


def _attend_cached(qa, ka, va, qb, kb, vb, core_a, core_b, k_a_c, v_a_c, k_b_c, v_b_c):
    return (_with_cache(core_a, qa, ka, va, k_a_c, v_a_c),
            _with_cache(core_b, qb, kb, vb, k_b_c, v_b_c))


def _moe(h, w_router, b_router, w_up, b_up, w_down, b_down):
    lead = h.shape[:-1]
    hf = h.reshape(-1, D_MODEL)
    n_tok = hf.shape[0]
    n_assign = n_tok * TOP_K
    logits = (hf @ w_router + b_router).astype(jnp.float32)
    top_logit, top_e = lax.top_k(logits, TOP_K)
    gate = jax.nn.softmax(top_logit, axis=-1).astype(h.dtype)
    flat_e = top_e.reshape(n_assign)
    flat_t = jnp.arange(n_assign, dtype=jnp.int32) // TOP_K
    flat_g = gate.reshape(n_assign)
    order = jnp.argsort(flat_e)
    se = flat_e[order]
    counts = jnp.bincount(flat_e, length=N_EXPERTS)
    padded = (counts + EXPERT_BLOCK - 1) // EXPERT_BLOCK * EXPERT_BLOCK
    p_end = jnp.cumsum(padded)
    p_start = p_end - padded
    start = jnp.cumsum(counts) - counts
    dest = p_start[se] + jnp.arange(n_assign, dtype=jnp.int32) - start[se]
    n_blocks = -(-n_assign // EXPERT_BLOCK) + N_EXPERTS
    n_rows = n_blocks * EXPERT_BLOCK
    row_tok = jnp.full((n_rows,), n_tok, jnp.int32).at[dest].set(flat_t[order])
    row_gate = jnp.zeros((n_rows,), h.dtype).at[dest].set(flat_g[order])
    block_e = jnp.minimum(
        jnp.searchsorted(p_end, jnp.arange(n_blocks, dtype=jnp.int32) * EXPERT_BLOCK, side='right'),
        N_EXPERTS - 1)
    hpad = jnp.concatenate([hf, jnp.zeros((1, D_MODEL), hf.dtype)], axis=0)
    xb_all = hpad[row_tok].reshape(n_blocks, EXPERT_BLOCK, D_MODEL)

    def expert_block(args):
        xb, e = args
        u = xb @ w_up[e] + b_up[e]
        glu = jnp.minimum(u[:, :D_FF], SWIGLU_LIMIT)
        lin = jnp.clip(u[:, D_FF:], -SWIGLU_LIMIT, SWIGLU_LIMIT)
        a = glu * jax.nn.sigmoid(SWIGLU_ALPHA * glu) * (lin + 1.0)
        return a @ w_down[e] + b_down[e]

    yb = lax.map(expert_block, (xb_all, block_e)).reshape(n_rows, D_MODEL)
    out = jax.ops.segment_sum(yb * row_gate[:, None], row_tok, num_segments=n_tok + 1)[:n_tok]
    return out.reshape(lead + (D_MODEL,))


def _layer(x, c, attend, w_ada, b_ada, g_attn, g_ffn, w_in, b_in, g_q_a, g_k_a, g_q_b, g_k_b,
           w_out, b_out, w_router, b_router, w_up, b_up, w_down, b_down):
    b, t = x.shape[0], x.shape[1]
    mod = (jax.nn.silu(c) @ w_ada + b_ada)[:, None, :]
    shift1, scale1, gate1, shift2, scale2, gate2 = jnp.split(mod, N_MOD, axis=-1)
    h = _rms(x, g_attn) * (1.0 + scale1) + shift1
    proj = h @ w_in + b_in
    qa, ka, va, qb, kb, vb = jnp.split(proj, SPLIT_POINTS, axis=-1)
    heads = lambda z: z.reshape(b, t, -1, HEAD_DIM)
    qa = _rms(heads(qa), g_q_a)
    ka = _rms(heads(ka), g_k_a)
    va = heads(va)
    qb = _rms(heads(qb), g_q_b)
    kb = _rms(heads(kb), g_k_b)
    vb = heads(vb)
    oa, ob = attend(qa, ka, va, qb, kb, vb)
    x = x + gate1 * (jnp.concatenate([oa, ob], axis=-1) @ w_out + b_out)
    h = _rms(x, g_ffn) * (1.0 + scale2) + shift2
    x = x + gate2 * _moe(h, w_router, b_router, w_up, b_up, w_down, b_down)
    return x, ka, va, kb, vb


def setup_inputs(seed: int = 0) -> dict:
    key = jax.random.key(seed)
    ks = jax.random.split(key, 32)
    nrm = lambda k, shape, s: jax.random.normal(k, shape, jnp.float32) * s
    la = min(BAND_ROWS_A, PAST_LEN)
    lb = min(WINDOW_B, PAST_LEN)
    d = D_MODEL
    return {
        'x_prompt': nrm(ks[0], (BATCH, SEQ, d), 1.0),
        'x_sample': nrm(ks[1], (DEC_BATCH, DEC_SEQ, d), 1.0),
        'c_prompt': nrm(ks[2], (BATCH, d), 1.0),
        'c_sample': nrm(ks[3], (DEC_BATCH, d), 1.0),
        'cache_k_a': nrm(ks[4], (DEPTH, DEC_BATCH, la, HEADS_A, HEAD_DIM), 1.0),
        'cache_v_a': nrm(ks[5], (DEPTH, DEC_BATCH, la, HEADS_A, HEAD_DIM), 1.0),
        'cache_k_b': nrm(ks[6], (DEPTH, DEC_BATCH, lb, KV_HEADS_B, HEAD_DIM), 1.0),
        'cache_v_b': nrm(ks[7], (DEPTH, DEC_BATCH, lb, KV_HEADS_B, HEAD_DIM), 1.0),
        'w_ada': nrm(ks[8], (DEPTH, d, N_MOD * d), 0.5 * d ** -0.5),
        'b_ada': nrm(ks[9], (DEPTH, N_MOD * d), 0.01),
        'g_attn': 1.0 + nrm(ks[10], (DEPTH, d), 0.02),
        'g_ffn': 1.0 + nrm(ks[11], (DEPTH, d), 0.02),
        'w_in': nrm(ks[12], (DEPTH, d, IN_WIDTH), d ** -0.5),
        'b_in': nrm(ks[13], (DEPTH, IN_WIDTH), 0.01),
        'g_q_a': 1.0 + nrm(ks[14], (DEPTH, HEAD_DIM), 0.02),
        'g_k_a': 1.0 + nrm(ks[15], (DEPTH, HEAD_DIM), 0.02),
        'g_q_b': 1.0 + nrm(ks[16], (DEPTH, HEAD_DIM), 0.02),
        'g_k_b': 1.0 + nrm(ks[17], (DEPTH, HEAD_DIM), 0.02),
        'rel_bias_a': nrm(ks[18], (DEPTH, HEADS_A, 2 * REL_CLIP + 1), 0.5),
        'sinks_b': nrm(ks[19], (DEPTH, HEADS_B), 1.0),
        'w_out': nrm(ks[20], (DEPTH, MIX_WIDTH, d), MIX_WIDTH ** -0.5),
        'b_out': nrm(ks[21], (DEPTH, d), 0.01),
        'w_router': nrm(ks[22], (DEPTH, d, N_EXPERTS), d ** -0.5),
        'b_router': nrm(ks[23], (DEPTH, N_EXPERTS), 0.01),
        'w_up': nrm(ks[24], (DEPTH, N_EXPERTS, d, 2 * D_FF), d ** -0.5),
        'b_up': nrm(ks[25], (DEPTH, N_EXPERTS, 2 * D_FF), 0.01),
        'w_down': nrm(ks[26], (DEPTH, N_EXPERTS, D_FF, d), D_FF ** -0.5),
        'b_down': nrm(ks[27], (DEPTH, N_EXPERTS, d), 0.01),
    }


def reference(x_prompt, x_sample, c_prompt, c_sample, cache_k_a, cache_v_a, cache_k_b, cache_v_b,
              w_ada, b_ada, g_attn, g_ffn, w_in, b_in, g_q_a, g_k_a, g_q_b, g_k_b,
              rel_bias_a, sinks_b, w_out, b_out, w_router, b_router, w_up, b_up, w_down, b_down):
    xp = x_prompt
    xs = x_sample
    kap, vap, kbp, vbp = [], [], [], []
    kas, vas, kbs, vbs = [], [], [], []
    for l in range(DEPTH):
        core_a = functools.partial(_attn_relbias, rel_bias=rel_bias_a[l])
        core_b = functools.partial(_attn_sink_alibi, sinks=sinks_b[l])
        attend_p = functools.partial(_attend_prompt, core_a=core_a, core_b=core_b)
        attend_s = functools.partial(_attend_cached, core_a=core_a, core_b=core_b,
                                     k_a_c=cache_k_a[l], v_a_c=cache_v_a[l],
                                     k_b_c=cache_k_b[l], v_b_c=cache_v_b[l])
        lw = (w_ada[l], b_ada[l], g_attn[l], g_ffn[l], w_in[l], b_in[l], g_q_a[l], g_k_a[l],
              g_q_b[l], g_k_b[l], w_out[l], b_out[l], w_router[l], b_router[l],
              w_up[l], b_up[l], w_down[l], b_down[l])
        xp, ka, va, kb, vb = _layer(xp, c_prompt, attend_p, *lw)
        kap.append(ka[:, -BAND_ROWS_A:])
        vap.append(va[:, -BAND_ROWS_A:])
        kbp.append(kb[:, -WINDOW_B:])
        vbp.append(vb[:, -WINDOW_B:])
        xs, ka, va, kb, vb = _layer(xs, c_sample, attend_s, *lw)
        kas.append(ka)
        vas.append(va)
        kbs.append(kb)
        vbs.append(vb)
    return (xp, xs, jnp.stack(kap), jnp.stack(vap), jnp.stack(kbp), jnp.stack(vbp),
            jnp.stack(kas), jnp.stack(vas), jnp.stack(kbs), jnp.stack(vbs))
```

```python
import functools
import math

import jax
import jax.numpy as jnp
from jax import lax
from jax.experimental import pallas as pl
from jax.experimental.pallas import tpu as pltpu

CHUNK = 64
HEAD_DIM = 64
LANES = 128
TOP_K = 4
BAND_CHUNKS_A = 8
BAND_CHUNKS_B = 2
REL_CLIP = 128
PAST_LEN = 1024
RMS_EPS = 1e-5
SWIGLU_ALPHA = 1.702
SWIGLU_LIMIT = 7.0
SCALE = HEAD_DIM ** -0.5
NEG = -1e30
N_MOD = 6
VMEM_LIMIT = 56 * 1024 * 1024

F32 = jnp.float32
BF16 = jnp.bfloat16


def _cparams(sem):
    return pltpu.CompilerParams(dimension_semantics=sem, vmem_limit_bytes=VMEM_LIMIT)


def _pick(n, cands):
    for c in cands:
        if n % c == 0:
            return c
    raise ValueError(f"no tile in {cands} divides {n}")


def _nt_dot(a, b):
    return lax.dot_general(a, b, (((1,), (1,)), ((), ())), preferred_element_type=F32)


def _ada_kernel(c_ref, w_ref, b_ref, o_ref):
    c = c_ref[...]
    s = (c * (1.0 / (1.0 + jnp.exp(-c)))).astype(BF16)
    o_ref[...] = jnp.dot(s, w_ref[...].astype(BF16), preferred_element_type=F32) + b_ref[...]


def _ada(c_pad, w_ada, b_ada):
    depth, d, n = w_ada.shape
    rows = c_pad.shape[0]
    tn = _pick(n, (512, 256, 128))
    return pl.pallas_call(
        _ada_kernel,
        grid=(depth, n // tn),
        in_specs=[
            pl.BlockSpec((rows, d), lambda l, j: (0, 0)),
            pl.BlockSpec((None, d, tn), lambda l, j: (l, 0, j)),
            pl.BlockSpec((None, 1, tn), lambda l, j: (l, 0, j)),
        ],
        out_specs=pl.BlockSpec((None, rows, tn), lambda l, j: (l, 0, j)),
        out_shape=jax.ShapeDtypeStruct((depth, rows, n), F32),
        compiler_params=_cparams(("arbitrary", "arbitrary")),
        name="ada_mod",
    )(c_pad, w_ada, b_ada.reshape(depth, 1, n))


def _rms_mod(x, g, scale, shift):
    tm, d = x.shape
    ms = jnp.mean(x * x, axis=-1, keepdims=True)
    y = x * lax.rsqrt(ms + RMS_EPS) * g
    y = y.reshape(tm // CHUNK, CHUNK, d)
    return (y * (1.0 + scale) + shift).reshape(tm, d)


def _norm_kernel(x_ref, g_ref, scale_ref, shift_ref, h_ref):
    h_ref[...] = _rms_mod(x_ref[...], g_ref[...], scale_ref[...], shift_ref[...]).astype(BF16)


def _norm(x, g_all, layer, modc, scale_col, shift_col):
    n, d = x.shape
    tm = _pick(n, (256, 128, 64))
    cpt = tm // CHUNK
    return pl.pallas_call(
        _norm_kernel,
        grid=(n // tm,),
        in_specs=[
            pl.BlockSpec((tm, d), lambda i: (i, 0)),
            pl.BlockSpec((None, 1, d), lambda i: (layer, 0, 0)),
            pl.BlockSpec((cpt, 1, d), lambda i: (i, 0, scale_col)),
            pl.BlockSpec((cpt, 1, d), lambda i: (i, 0, shift_col)),
        ],
        out_specs=pl.BlockSpec((tm, d), lambda i: (i, 0)),
        out_shape=jax.ShapeDtypeStruct((n, d), BF16),
        compiler_params=_cparams(("arbitrary",)),
        name="norm_mod",
    )(x, g_all, modc, modc)


def _inproj_kernel(flag_ref, h_ref, w_ref, b_ref, gain_ref, grp_ref, o_ref, wbf_ref):
    j = pl.program_id(0)

    @pl.when(pl.program_id(1) == 0)
    def _():
        wbf_ref[...] = w_ref[...].astype(BF16)

    y = jnp.dot(h_ref[...], wbf_ref[...], preferred_element_type=F32) + b_ref[...]

    @pl.when(flag_ref[j] == 0)
    def _():
        o_ref[...] = y

    @pl.when(flag_ref[j] != 0)
    def _():
        sq = y * y
        hi = sq.astype(BF16)
        lo = (sq - hi.astype(F32)).astype(BF16)
        grp = grp_ref[...]
        ss = (jnp.dot(hi, grp, preferred_element_type=F32)
              + jnp.dot(lo, grp, preferred_element_type=F32))
        o_ref[...] = y * lax.rsqrt(ss * (1.0 / HEAD_DIM) + RMS_EPS) * gain_ref[...]


def _inproj(h, w_in, b_in, gain, flags, layer, tn):
    n, d = h.shape
    width = w_in.shape[-1]
    tm = _pick(n, (512, 256, 128))
    grp = (jnp.arange(tn)[:, None] // HEAD_DIM == jnp.arange(tn)[None, :] // HEAD_DIM).astype(BF16)
    grid_spec = pltpu.PrefetchScalarGridSpec(
        num_scalar_prefetch=1,
        grid=(width // tn, n // tm),
        in_specs=[
            pl.BlockSpec((tm, d), lambda j, i, f: (i, 0)),
            pl.BlockSpec((None, d, tn), lambda j, i, f: (layer, 0, j)),
            pl.BlockSpec((None, 1, tn), lambda j, i, f: (layer, 0, j)),
            pl.BlockSpec((None, 1, tn), lambda j, i, f: (layer, 0, j)),
            pl.BlockSpec((tn, tn), lambda j, i, f: (0, 0)),
        ],
        out_specs=pl.BlockSpec((tm, tn), lambda j, i, f: (i, j)),
        scratch_shapes=[pltpu.VMEM((d, tn), BF16)],
    )
    return pl.pallas_call(
        _inproj_kernel,
        grid_spec=grid_spec,
        out_shape=jax.ShapeDtypeStruct((n, width), F32),
        compiler_params=_cparams(("arbitrary", "arbitrary")),
        name="in_proj",
    )(flags, h, w_in, b_in, gain, grp)


def _softmax_pv(scores, values, sink=None):
    m = None
    for s in scores:
        mj = jnp.max(s, axis=-1, keepdims=True)
        m = mj if m is None else jnp.maximum(m, mj)
    if sink is not None:
        m = jnp.maximum(m, sink)
    den = None
    out = None
    for s, v in zip(scores, values):
        p = jnp.exp(s - m)
        dj = jnp.sum(p, axis=-1, keepdims=True)
        oj = jnp.dot(p.astype(BF16), v, preferred_element_type=F32)
        den = dj if den is None else den + dj
        out = oj if out is None else out + oj
    if sink is not None:
        den = den + jnp.exp(sink - m)
    return out / den


def _stack_pair_a(q):
    lo = lax.broadcasted_iota(jnp.int32, q.shape, 1) < HEAD_DIM
    return jnp.concatenate([jnp.where(lo, q, 0.0), jnp.where(lo, 0.0, q)], axis=0).astype(BF16)


def _unstack_pair_a(o, rows):
    lo = lax.broadcasted_iota(jnp.int32, (rows, LANES), 1) < HEAD_DIM
    return jnp.where(lo, o[:rows], o[rows:])


def _stack_group_b(q, n_heads):
    rows = q.shape[0]
    lo = lax.broadcasted_iota(jnp.int32, (rows, LANES), 1) < HEAD_DIM
    per_kv = n_heads // 2
    parts = []
    for i in range(n_heads):
        blk = q[:, (i // 2) * LANES:(i // 2 + 1) * LANES]
        src_half, kv_half = i % 2, i // per_kv
        if src_half != kv_half:
            blk = pltpu.roll(blk, HEAD_DIM, 1)
        parts.append(jnp.where(lo if kv_half == 0 else jnp.logical_not(lo), blk, 0.0))
    return jnp.concatenate(parts, axis=0).astype(BF16)


def _unstack_group_b(o, rows, n_heads):
    lo = lax.broadcasted_iota(jnp.int32, (rows, LANES), 1) < HEAD_DIM
    per_kv = n_heads // 2
    cols = []
    for mblk in range(n_heads // 2):
        halves = []
        for u in range(2):
            i = 2 * mblk + u
            part = o[i * rows:(i + 1) * rows]
            if i // per_kv != u:
                part = pltpu.roll(part, HEAD_DIM, 1)
            halves.append(part)
        cols.append(jnp.where(lo, halves[0], halves[1]))
    return jnp.concatenate(cols, axis=1)


def _attn_a_prompt_kernel(q_ref, k_ref, v_ref, bias_ref, o_ref, *, rq, n_prev):
    qb = pl.program_id(2)
    qs = _stack_pair_a(q_ref[...] * SCALE)
    scores, values = [], []
    for j in range(n_prev + 1):
        blk = qb - n_prev + j
        start = pl.multiple_of(jnp.maximum(blk, 0) * rq, rq)
        kj = k_ref[pl.ds(start, rq), :].astype(BF16)
        values.append(v_ref[pl.ds(start, rq), :].astype(BF16))
        s = _nt_dot(qs, kj) + bias_ref[j]
        if j < n_prev:
            s = jnp.where(blk >= 0, s, NEG)
        scores.append(s)
    o = _softmax_pv(scores, values)
    o_ref[...] = _unstack_pair_a(o, rq).astype(BF16)


def _attn_a_prompt(proj, bias, batch, seq, width_a, rq):
    n_pairs = width_a // LANES
    n_prev = BAND_CHUNKS_A * CHUNK // rq
    nqb = seq // rq
    kcol, vcol = width_a // LANES, 2 * width_a // LANES
    return pl.pallas_call(
        functools.partial(_attn_a_prompt_kernel, rq=rq, n_prev=n_prev),
        grid=(batch, n_pairs, nqb),
        in_specs=[
            pl.BlockSpec((rq, LANES), lambda b, p, i: (b * nqb + i, p)),
            pl.BlockSpec((seq, LANES), lambda b, p, i: (b, kcol + p)),
            pl.BlockSpec((seq, LANES), lambda b, p, i: (b, vcol + p)),
            pl.BlockSpec((None, n_prev + 1, 2 * rq, rq), lambda b, p, i: (p, 0, 0, 0)),
        ],
        out_specs=pl.BlockSpec((rq, LANES), lambda b, p, i: (b * nqb + i, p)),
        out_shape=jax.ShapeDtypeStruct((batch * seq, width_a), BF16),
        compiler_params=_cparams(("arbitrary", "arbitrary", "arbitrary")),
        name="attn_a_prompt",
    )(proj, proj, proj, bias)


def _attn_a_sample_kernel(q_ref, kn_ref, vn_ref, kc_ref, vc_ref, bias_ref, o_ref, *, n_cache):
    qs = _stack_pair_a(q_ref[...] * SCALE)
    bias = bias_ref[...]
    s_c = _nt_dot(qs, kc_ref[...].astype(BF16)) + bias[:, :n_cache]
    s_n = _nt_dot(qs, kn_ref[...].astype(BF16)) + bias[:, n_cache:]
    o = _softmax_pv([s_c, s_n], [vc_ref[...].astype(BF16), vn_ref[...].astype(BF16)])
    o_ref[...] = _unstack_pair_a(o, q_ref.shape[0]).astype(BF16)


def _attn_a_sample(proj, cache_k, cache_v, bias, layer, row0, dec_batch, dec_seq, width_a):
    n_pairs = width_a // LANES
    n_cache = cache_k.shape[2]
    rb0 = row0 // dec_seq
    kcol, vcol = width_a // LANES, 2 * width_a // LANES
    ck = cache_k.reshape(cache_k.shape[0], dec_batch, n_cache, width_a)
    cv = cache_v.reshape(cache_v.shape[0], dec_batch, n_cache, width_a)
    return pl.pallas_call(
        functools.partial(_attn_a_sample_kernel, n_cache=n_cache),
        grid=(dec_batch, n_pairs),
        in_specs=[
            pl.BlockSpec((dec_seq, LANES), lambda b, p: (rb0 + b, p)),
            pl.BlockSpec((dec_seq, LANES), lambda b, p: (rb0 + b, kcol + p)),
            pl.BlockSpec((dec_seq, LANES), lambda b, p: (rb0 + b, vcol + p)),
            pl.BlockSpec((None, None, n_cache, LANES), lambda b, p: (layer, b, 0, p)),
            pl.BlockSpec((None, None, n_cache, LANES), lambda b, p: (layer, b, 0, p)),
            pl.BlockSpec((None, 2 * dec_seq, n_cache + dec_seq), lambda b, p: (p, 0, 0)),
        ],
        out_specs=pl.BlockSpec((dec_seq, LANES), lambda b, p: (b, p)),
        out_shape=jax.ShapeDtypeStruct((dec_batch * dec_seq, width_a), BF16),
        compiler_params=_cparams(("arbitrary", "arbitrary")),
        name="attn_a_sample",
    )(proj, proj, proj, ck, cv, bias)


def _attn_b_prompt_kernel(q_ref, k_ref, v_ref, bias_ref, sink_ref, o_ref, *, rq, n_heads):
    qb = pl.program_id(2)
    qs = _stack_group_b(q_ref[...] * SCALE, n_heads)
    bias = bias_ref[...]
    prev = qb - 1
    p_start = pl.multiple_of(jnp.maximum(prev, 0) * rq, rq)
    o_start = pl.multiple_of(qb * rq, rq)
    s_p = _nt_dot(qs, k_ref[pl.ds(p_start, rq), :].astype(BF16)) + bias[:, :rq]
    s_p = jnp.where(prev >= 0, s_p, NEG)
    s_o = _nt_dot(qs, k_ref[pl.ds(o_start, rq), :].astype(BF16)) + bias[:, rq:]
    o = _softmax_pv([s_p, s_o],
                    [v_ref[pl.ds(p_start, rq), :].astype(BF16),
                     v_ref[pl.ds(o_start, rq), :].astype(BF16)],
                    sink_ref[...])
    o_ref[...] = _unstack_group_b(o, rq, n_heads).astype(BF16)


def _attn_b_prompt(proj, bias, sink, batch, seq, qcol0, kcol0, vcol0, width_b, n_kv_pairs, rq):
    n_heads = width_b // HEAD_DIM // n_kv_pairs
    qw = n_heads * HEAD_DIM
    nqb = seq // rq
    qc, kc, vc = qcol0 // qw, kcol0 // LANES, vcol0 // LANES
    return pl.pallas_call(
        functools.partial(_attn_b_prompt_kernel, rq=rq, n_heads=n_heads),
        grid=(batch, n_kv_pairs, nqb),
        in_specs=[
            pl.BlockSpec((rq, qw), lambda b, p, i: (b * nqb + i, qc + p)),
            pl.BlockSpec((seq, LANES), lambda b, p, i: (b, kc + p)),
            pl.BlockSpec((seq, LANES), lambda b, p, i: (b, vc + p)),
            pl.BlockSpec((None, n_heads * rq, 2 * rq), lambda b, p, i: (p, 0, 0)),
            pl.BlockSpec((None, n_heads * rq, 1), lambda b, p, i: (p, 0, 0)),
        ],
        out_specs=pl.BlockSpec((rq, qw), lambda b, p, i: (b * nqb + i, p)),
        out_shape=jax.ShapeDtypeStruct((batch * seq, width_b), BF16),
        compiler_params=_cparams(("arbitrary", "arbitrary", "arbitrary")),
        name="attn_b_prompt",
    )(proj, proj, proj, bias, sink)


def _attn_b_sample_kernel(q_ref, kn_ref, vn_ref, kc_ref, vc_ref, bias_ref, sink_ref, o_ref, *,
                          n_cache, n_heads):
    qs = _stack_group_b(q_ref[...] * SCALE, n_heads)
    bias = bias_ref[...]
    s_c = _nt_dot(qs, kc_ref[...].astype(BF16)) + bias[:, :n_cache]
    s_n = _nt_dot(qs, kn_ref[...].astype(BF16)) + bias[:, n_cache:]
    o = _softmax_pv([s_c, s_n], [vc_ref[...].astype(BF16), vn_ref[...].astype(BF16)],
                    sink_ref[...])
    o_ref[...] = _unstack_group_b(o, q_ref.shape[0], n_heads).astype(BF16)


def _attn_b_sample(proj, cache_k, cache_v, bias, sink, layer, row0, dec_batch, dec_seq,
                   qcol0, kcol0, vcol0, width_b, n_kv_pairs):
    n_heads = width_b // HEAD_DIM // n_kv_pairs
    qw = n_heads * HEAD_DIM
    n_cache = cache_k.shape[2]
    kvw = cache_k.shape[3] * cache_k.shape[4]
    rb0 = row0 // dec_seq
    qc, kc, vc = qcol0 // qw, kcol0 // LANES, vcol0 // LANES
    ck = cache_k.reshape(cache_k.shape[0], dec_batch, n_cache, kvw)
    cv = cache_v.reshape(cache_v.shape[0], dec_batch, n_cache, kvw)
    return pl.pallas_call(
        functools.partial(_attn_b_sample_kernel, n_cache=n_cache, n_heads=n_heads),
        grid=(dec_batch, n_kv_pairs),
        in_specs=[
            pl.BlockSpec((dec_seq, qw), lambda b, p: (rb0 + b, qc + p)),
            pl.BlockSpec((dec_seq, LANES), lambda b, p: (rb0 + b, kc + p)),
            pl.BlockSpec((dec_seq, LANES), lambda b, p: (rb0 + b, vc + p)),
            pl.BlockSpec((None, None, n_cache, LANES), lambda b, p: (layer, b, 0, p)),
            pl.BlockSpec((None, None, n_cache, LANES), lambda b, p: (layer, b, 0, p)),
            pl.BlockSpec((None, n_heads * dec_seq, n_cache + dec_seq), lambda b, p: (p, 0, 0)),
            pl.BlockSpec((None, n_heads * dec_seq, 1), lambda b, p: (p, 0, 0)),
        ],
        out_specs=pl.BlockSpec((dec_seq, qw), lambda b, p: (b, p)),
        out_shape=jax.ShapeDtypeStruct((dec_batch * dec_seq, width_b), BF16),
        compiler_params=_cparams(("arbitrary", "arbitrary")),
        name="attn_b_sample",
    )(proj, proj, proj, ck, cv, bias, sink)


def _outproj_kernel(a_ref, w_ref, b_ref, x_ref, gate_ref, o_ref, wbf_ref):
    @pl.when(pl.program_id(1) == 0)
    def _():
        wbf_ref[...] = w_ref[...].astype(BF16)

    y = jnp.dot(a_ref[...], wbf_ref[...], preferred_element_type=F32) + b_ref[...]
    tm, tn = y.shape
    y = y.reshape(tm // CHUNK, CHUNK, tn) * gate_ref[...]
    o_ref[...] = x_ref[...] + y.reshape(tm, tn)


def _outproj(attn, w_out, b_out, x, modc, gate_col, layer):
    n, kdim = attn.shape
    d = w_out.shape[-1]
    tm = _pick(n, (512, 256, 128))
    tn = _pick(d, (512, 256, 128))
    cpt = tm // CHUNK
    gc = gate_col * (d // tn)
    return pl.pallas_call(
        _outproj_kernel,
        grid=(d // tn, n // tm),
        in_specs=[
            pl.BlockSpec((tm, kdim), lambda j, i: (i, 0)),
            pl.BlockSpec((None, kdim, tn), lambda j, i: (layer, 0, j)),
            pl.BlockSpec((None, 1, tn), lambda j, i: (layer, 0, j)),
            pl.BlockSpec((tm, tn), lambda j, i: (i, j)),
            pl.BlockSpec((cpt, 1, tn), lambda j, i: (i, 0, gc + j)),
        ],
        out_specs=pl.BlockSpec((tm, tn), lambda j, i: (i, j)),
        out_shape=jax.ShapeDtypeStruct((n, d), F32),
        scratch_shapes=[pltpu.VMEM((kdim, tn), BF16)],
        compiler_params=_cparams(("arbitrary", "arbitrary")),
        name="out_proj",
    )(attn, w_out, b_out, x, modc)


def _router_kernel(x_ref, g_ref, scale_ref, shift_ref, wr_ref, br_ref,
                   hp_ref, e_ref, gate_ref, whi_ref, wlo_ref, *, n_experts):
    @pl.when(pl.program_id(0) == 0)
    def _():
        w = wr_ref[...]
        hi = w.astype(BF16)
        whi_ref[...] = hi
        wlo_ref[...] = (w - hi.astype(F32)).astype(BF16)

    h = _rms_mod(x_ref[...], g_ref[...], scale_ref[...], shift_ref[...])
    tm, d = h.shape
    hi = h.astype(BF16)
    hi32 = hi.astype(F32)
    lo = (h - hi32).astype(BF16)
    bits = lax.bitcast_convert_type(hi32, jnp.uint32)
    hp_ref[...] = bits[:, :d // 2] | (bits[:, d // 2:] >> 16)

    logits = (jnp.dot(hi, whi_ref[...], preferred_element_type=F32)
              + jnp.dot(lo, whi_ref[...], preferred_element_type=F32)
              + jnp.dot(hi, wlo_ref[...], preferred_element_type=F32)) + br_ref[...]

    lane_e = lax.broadcasted_iota(jnp.int32, (tm, n_experts), 1).astype(F32)
    lane_o = lax.broadcasted_iota(jnp.int32, (tm, LANES), 1)
    work = logits
    tops, idxs = [], []
    for _ in range(TOP_K):
        mx = jnp.max(work, axis=-1, keepdims=True)
        ix = jnp.min(jnp.where(work == mx, lane_e, float(n_experts)), axis=-1, keepdims=True)
        tops.append(mx)
        idxs.append(ix)
        work = jnp.where(lane_e == ix, -jnp.inf, work)
    exps = [jnp.exp(t - tops[0]) for t in tops]
    den = exps[0] + exps[1] + exps[2] + exps[3]
    e_out = jnp.zeros((tm, LANES), F32)
    g_out = jnp.zeros((tm, LANES), F32)
    for k in range(TOP_K):
        e_out = jnp.where(lane_o == k, idxs[k], e_out)
        g_out = jnp.where(lane_o == k, exps[k] / den, g_out)
    e_ref[...] = e_out.astype(jnp.int32)
    gate_ref[...] = g_out


def _router(x, g_all, layer, modc, scale_col, shift_col, w_router, b_router):
    n, d = x.shape
    n_experts = w_router.shape[-1]
    tm = _pick(n, (256, 128, 64))
    cpt = tm // CHUNK
    depth = w_router.shape[0]
    return pl.pallas_call(
        functools.partial(_router_kernel, n_experts=n_experts),
        grid=(n // tm,),
        in_specs=[
            pl.BlockSpec((tm, d), lambda i: (i, 0)),
            pl.BlockSpec((None, 1, d), lambda i: (layer, 0, 0)),
            pl.BlockSpec((cpt, 1, d), lambda i: (i, 0, scale_col)),
            pl.BlockSpec((cpt, 1, d), lambda i: (i, 0, shift_col)),
            pl.BlockSpec((None, d, n_experts), lambda i: (layer, 0, 0)),
            pl.BlockSpec((None, 1, n_experts), lambda i: (layer, 0, 0)),
        ],
        out_specs=[
            pl.BlockSpec((tm, d // 2), lambda i: (i, 0)),
            pl.BlockSpec((tm, LANES), lambda i: (i, 0)),
            pl.BlockSpec((tm, LANES), lambda i: (i, 0)),
        ],
        out_shape=[
            jax.ShapeDtypeStruct((n, d // 2), jnp.uint32),
            jax.ShapeDtypeStruct((n, LANES), jnp.int32),
            jax.ShapeDtypeStruct((n, LANES), F32),
        ],
        scratch_shapes=[pltpu.VMEM((d, n_experts), BF16), pltpu.VMEM((d, n_experts), BF16)],
        compiler_params=_cparams(("arbitrary",)),
        name="norm_router",
    )(x, g_all, modc, modc, w_router, b_router.reshape(depth, 1, n_experts))


def _row_copy(src_hbm, dst_vmem, sem, src_row, dst_row):
    return pltpu.make_async_copy(src_hbm.at[pl.ds(src_row, 1)], dst_vmem.at[pl.ds(dst_row, 1)], sem)


def _gather_kernel(tok_ref, h_hbm, o_ref, sem, *, tm):
    base = pl.program_id(0) * tm

    def issue(r, carry):
        _row_copy(h_hbm, o_ref, sem, tok_ref[base + r], r).start()
        return carry

    lax.fori_loop(0, tm, issue, 0)

    def drain(r, carry):
        _row_copy(h_hbm, o_ref, sem, 0, r).wait()
        return carry

    lax.fori_loop(0, tm, drain, 0)


def _gather_rows(row_tok, hp, tm):
    n_rows = row_tok.shape[0]
    width = hp.shape[1]
    grid_spec = pltpu.PrefetchScalarGridSpec(
        num_scalar_prefetch=1,
        grid=(n_rows // tm,),
        in_specs=[pl.BlockSpec(memory_space=pl.ANY)],
        out_specs=pl.BlockSpec((tm, width), lambda i, t: (i, 0)),
        scratch_shapes=[pltpu.SemaphoreType.DMA],
    )
    return pl.pallas_call(
        functools.partial(_gather_kernel, tm=tm),
        grid_spec=grid_spec,
        out_shape=jax.ShapeDtypeStruct((n_rows, width), hp.dtype),
        compiler_params=_cparams(("arbitrary",)),
        name="moe_gather",
    )(row_tok, hp)


def _unpack_rows(words):
    lo = lax.bitcast_convert_type(words & jnp.uint32(0xFFFF0000), F32).astype(BF16)
    hi = lax.bitcast_convert_type(words << 16, F32).astype(BF16)
    return lo, hi


def _expert_changed(te_ref, i):
    return jnp.logical_or(i == 0, te_ref[i] != te_ref[jnp.maximum(i - 1, 0)])


def _up_kernel(te_ref, nu_ref, x_ref, wg_ref, wl_ref, bg_ref, bl_ref, o_ref, wgb_ref, wlb_ref):
    i = pl.program_id(1)

    @pl.when(_expert_changed(te_ref, i))
    def _():
        wgb_ref[...] = wg_ref[...].astype(BF16)
        wlb_ref[...] = wl_ref[...].astype(BF16)

    @pl.when(i < nu_ref[0])
    def _():
        x0, x1 = _unpack_rows(x_ref[...])
        half = x0.shape[1]
        glu = (jnp.dot(x0, wgb_ref[:half, :], preferred_element_type=F32)
               + jnp.dot(x1, wgb_ref[half:, :], preferred_element_type=F32)) + bg_ref[...]
        lin = (jnp.dot(x0, wlb_ref[:half, :], preferred_element_type=F32)
               + jnp.dot(x1, wlb_ref[half:, :], preferred_element_type=F32)) + bl_ref[...]
        glu = jnp.minimum(glu, SWIGLU_LIMIT)
        lin = jnp.clip(lin, -SWIGLU_LIMIT, SWIGLU_LIMIT)
        act = glu * (1.0 / (1.0 + jnp.exp(-SWIGLU_ALPHA * glu))) * (lin + 1.0)
        o_ref[...] = act.astype(BF16)

    @pl.when(i >= nu_ref[0])
    def _():
        o_ref[...] = jnp.zeros_like(o_ref)


def _expert_up(tile_e, n_used, xs, w_up, b_up, layer, tm):
    n_rows, half = xs.shape
    depth, n_experts, d, two_ff = w_up.shape
    d_ff = two_ff // 2
    tn = _pick(d_ff, (512, 256, 128))
    nj = d_ff // tn
    b4 = b_up.reshape(depth, n_experts, 1, two_ff)
    grid_spec = pltpu.PrefetchScalarGridSpec(
        num_scalar_prefetch=2,
        grid=(nj, n_rows // tm),
        in_specs=[
            pl.BlockSpec((tm, half), lambda j, i, te, nu: (i, 0)),
            pl.BlockSpec((None, None, d, tn), lambda j, i, te, nu: (layer, te[i], 0, j)),
            pl.BlockSpec((None, None, d, tn), lambda j, i, te, nu: (layer, te[i], 0, nj + j)),
            pl.BlockSpec((None, None, 1, tn), lambda j, i, te, nu: (layer, te[i], 0, j)),
            pl.BlockSpec((None, None, 1, tn), lambda j, i, te, nu: (layer, te[i], 0, nj + j)),
        ],
        out_specs=pl.BlockSpec((tm, tn), lambda j, i, te, nu: (i, j)),
        scratch_shapes=[pltpu.VMEM((d, tn), BF16), pltpu.VMEM((d, tn), BF16)],
    )
    return pl.pallas_call(
        _up_kernel,
        grid_spec=grid_spec,
        out_shape=jax.ShapeDtypeStruct((n_rows, d_ff), BF16),
        compiler_params=_cparams(("arbitrary", "arbitrary")),
        name="expert_up",
    )(tile_e, n_used, xs, w_up, w_up, b4, b4)


def _down_kernel(te_ref, nu_ref, a_ref, w_ref, b_ref, o_ref, wb_ref):
    i = pl.program_id(1)

    @pl.when(_expert_changed(te_ref, i))
    def _():
        wb_ref[...] = w_ref[...].astype(BF16)

    @pl.when(i < nu_ref[0])
    def _():
        o_ref[...] = jnp.dot(a_ref[...], wb_ref[...], preferred_element_type=F32) + b_ref[...]

    @pl.when(i >= nu_ref[0])
    def _():
        o_ref[...] = jnp.zeros_like(o_ref)


def _expert_down(tile_e, n_used, act, w_down, b_down, layer, tm):
    n_rows, d_ff = act.shape
    depth, n_experts, _, d = w_down.shape
    tn = _pick(d, (1024, 512, 256, 128))
    b4 = b_down.reshape(depth, n_experts, 1, d)
    grid_spec = pltpu.PrefetchScalarGridSpec(
        num_scalar_prefetch=2,
        grid=(d // tn, n_rows // tm),
        in_specs=[
            pl.BlockSpec((tm, d_ff), lambda j, i, te, nu: (i, 0)),
            pl.BlockSpec((None, None, d_ff, tn), lambda j, i, te, nu: (layer, te[i], 0, j)),
            pl.BlockSpec((None, None, 1, tn), lambda j, i, te, nu: (layer, te[i], 0, j)),
        ],
        out_specs=pl.BlockSpec((tm, tn), lambda j, i, te, nu: (i, j)),
        scratch_shapes=[pltpu.VMEM((d_ff, tn), BF16)],
    )
    return pl.pallas_call(
        _down_kernel,
        grid_spec=grid_spec,
        out_shape=jax.ShapeDtypeStruct((n_rows, d), F32),
        compiler_params=_cparams(("arbitrary", "arbitrary")),
        name="expert_down",
    )(tile_e, n_used, act, w_down, b4)


def _combine_kernel(pos_ref, gate_ref, x_ref, g2_ref, y_hbm, o_ref, buf_ref, sem, *, tm):
    base = pl.program_id(0) * (tm * TOP_K)

    def issue(r, carry):
        for k in range(TOP_K):
            _row_copy(y_hbm, buf_ref.at[k], sem, pos_ref[base + r * TOP_K + k], r).start()
        return carry

    lax.fori_loop(0, tm, issue, 0)

    def drain(r, carry):
        for k in range(TOP_K):
            _row_copy(y_hbm, buf_ref.at[k], sem, 0, r).wait()
        return carry

    lax.fori_loop(0, tm, drain, 0)

    gate = gate_ref[...]
    acc = gate[:, 0:1] * buf_ref[0]
    for k in range(1, TOP_K):
        acc = acc + gate[:, k:k + 1] * buf_ref[k]
    d = acc.shape[1]
    acc = acc.reshape(tm // CHUNK, CHUNK, d) * g2_ref[...]
    o_ref[...] = x_ref[...] + acc.reshape(tm, d)


def _combine(pos, gate, x, modc, gate_col, yb):
    n, d = x.shape
    tm = _pick(n, (128, 64))
    cpt = tm // CHUNK
    grid_spec = pltpu.PrefetchScalarGridSpec(
        num_scalar_prefetch=1,
        grid=(n // tm,),
        in_specs=[
            pl.BlockSpec((tm, LANES), lambda i, p: (i, 0)),
            pl.BlockSpec((tm, d), lambda i, p: (i, 0)),
            pl.BlockSpec((cpt, 1, d), lambda i, p: (i, 0, gate_col)),
            pl.BlockSpec(memory_space=pl.ANY),
        ],
        out_specs=pl.BlockSpec((tm, d), lambda i, p: (i, 0)),
        scratch_shapes=[pltpu.VMEM((TOP_K, tm, d), F32), pltpu.SemaphoreType.DMA],
    )
    return pl.pallas_call(
        functools.partial(_combine_kernel, tm=tm),
        grid_spec=grid_spec,
        out_shape=jax.ShapeDtypeStruct((n, d), F32),
        compiler_params=_cparams(("arbitrary",)),
        name="moe_combine",
    )(pos, gate, x, modc, yb)


def _dispatch_plan(top_e, n_experts, tm):
    n_tok = top_e.shape[0]
    n_assign = n_tok * TOP_K
    flat_e = top_e.reshape(n_assign)
    onehot = (flat_e[:, None] == jnp.arange(n_experts, dtype=jnp.int32)[None, :]).astype(jnp.int32)
    csum = jnp.cumsum(onehot, axis=0)
    counts = csum[-1]
    rank = jnp.take_along_axis(csum, flat_e[:, None], axis=1)[:, 0] - 1
    padded = (counts + tm - 1) // tm * tm
    p_end = jnp.cumsum(padded)
    p_start = p_end - padded
    dest = (p_start[flat_e] + rank).astype(jnp.int32)
    n_tiles = n_assign // tm + n_experts
    row_tok = jnp.zeros((n_tiles * tm,), jnp.int32).at[dest].set(
        jnp.arange(n_assign, dtype=jnp.int32) // TOP_K)
    tile_start = jnp.arange(n_tiles, dtype=jnp.int32) * tm
    tile_e = jnp.minimum(jnp.searchsorted(p_end, tile_start, side="right"),
                         n_experts - 1).astype(jnp.int32)
    n_used = (p_end[-1] // tm).astype(jnp.int32).reshape(1)
    return dest, row_tok, tile_e, n_used


def _bias_a(rel_bias, rq, n_keys, masked):
    n_heads = rel_bias.shape[0]
    qi = jnp.arange(rq)[:, None]
    ki = jnp.arange(n_keys)[None, :]
    off = n_keys - rq
    rel = jnp.clip(ki - off - qi, -REL_CLIP, REL_CLIP) + REL_CLIP
    bias = jnp.take(rel_bias, rel, axis=1)
    if masked:
        qc = (qi + off) // CHUNK
        kc = ki // CHUNK
        ok = (kc <= qc) & (kc >= qc - BAND_CHUNKS_A)
        bias = jnp.where(ok[None], bias, NEG)
    return bias.reshape(n_heads // 2, 2 * rq, n_keys)


def _bias_b(n_heads_total, n_kv_pairs, rq, n_keys, masked):
    slopes = 2.0 ** (-8.0 * jnp.arange(1, n_heads_total + 1, dtype=F32) / n_heads_total)
    qi = jnp.arange(rq)[:, None]
    ki = jnp.arange(n_keys)[None, :]
    off = n_keys - rq
    dist = jnp.abs(qi - (ki - off)).astype(F32)
    bias = -slopes[:, None, None] * dist[None]
    if masked:
        qc = (qi + off) // CHUNK
        kc = ki // CHUNK
        ok = (kc <= qc) & (kc >= qc - BAND_CHUNKS_B)
        bias = jnp.where(ok[None], bias, NEG)
    return bias.reshape(n_kv_pairs, (n_heads_total // n_kv_pairs) * rq, n_keys)


def kernel(x_prompt, x_sample, c_prompt, c_sample, cache_k_a, cache_v_a, cache_k_b, cache_v_b,
           w_ada, b_ada, g_attn, g_ffn, w_in, b_in, g_q_a, g_k_a, g_q_b, g_k_b, rel_bias_a,
           sinks_b, w_out, b_out, w_router, b_router, w_up, b_up, w_down, b_down):
    batch, seq, d = x_prompt.shape
    dec_batch, dec_seq, _ = x_sample.shape
    depth = w_in.shape[0]
    heads_a = rel_bias_a.shape[1]
    heads_b = sinks_b.shape[1]
    kv_heads_b = cache_k_b.shape[3]
    n_experts = w_router.shape[-1]
    width_a, width_b, kv_width_b = heads_a * HEAD_DIM, heads_b * HEAD_DIM, kv_heads_b * HEAD_DIM
    in_width = w_in.shape[-1]
    n_cache_a, n_cache_b = cache_k_a.shape[2], cache_k_b.shape[2]
    band_a, band_b = BAND_CHUNKS_A * CHUNK, BAND_CHUNKS_B * CHUNK
    assert dec_seq == CHUNK and PAST_LEN % CHUNK == 0 and seq % CHUNK == 0
    assert n_cache_a == band_a and n_cache_b == band_b
    assert in_width == 3 * width_a + width_b + 2 * kv_width_b and width_a + width_b == d
    assert kv_heads_b % 2 == 0 and heads_a % 2 == 0

    n_p, n_s = batch * seq, dec_batch * dec_seq
    n_tok = n_p + n_s
    x = jnp.concatenate([x_prompt.reshape(n_p, d), x_sample.reshape(n_s, d)], axis=0)

    n_seq = batch + dec_batch
    c_all = jnp.concatenate([c_prompt, c_sample], axis=0)
    c_pad = jnp.pad(c_all, ((0, -n_seq % 8), (0, 0)))
    mod_all = _ada(c_pad, w_ada, b_ada)
    chunk_seq = jnp.concatenate([jnp.repeat(jnp.arange(batch), seq // CHUNK),
                                 batch + jnp.arange(dec_batch)])
    shift1_c, scale1_c, gate1_c, shift2_c, scale2_c, gate2_c = range(N_MOD)

    seg = (width_a, 2 * width_a, 3 * width_a, 3 * width_a + width_b,
           3 * width_a + width_b + kv_width_b, in_width)
    tn_in = _pick(math.gcd(*seg), (512, 256, 128))
    col = jnp.arange(in_width)
    is_v = ((col >= seg[1]) & (col < seg[2])) | (col >= seg[4])
    flags = jnp.logical_not(is_v[::tn_in]).astype(jnp.int32)

    def gains(l):
        ones_a, ones_kv = jnp.ones((width_a,), F32), jnp.ones((kv_width_b,), F32)
        return jnp.concatenate([jnp.tile(g_q_a[l], heads_a), jnp.tile(g_k_a[l], heads_a), ones_a,
                                jnp.tile(g_q_b[l], heads_b), jnp.tile(g_k_b[l], kv_heads_b),
                                ones_kv])

    gain_all = jnp.stack([gains(l) for l in range(depth)]).reshape(depth, 1, in_width)
    b_in3 = b_in.reshape(depth, 1, in_width)
    b_out3 = b_out.reshape(depth, 1, d)
    g_attn3 = g_attn.reshape(depth, 1, d)
    g_ffn3 = g_ffn.reshape(depth, 1, d)

    rq_a = 4 * CHUNK
    rq_b = 2 * CHUNK
    n_kv_pairs = kv_heads_b // 2
    heads_per_pair = heads_b // n_kv_pairs
    bias_bp = _bias_b(heads_b, n_kv_pairs, rq_b, 2 * rq_b, True)
    bias_bs = _bias_b(heads_b, n_kv_pairs, dec_seq, n_cache_b + dec_seq, False)

    tm_moe = 256
    new_ka_p, new_va_p, new_kb_p, new_vb_p = [], [], [], []
    new_ka_s, new_va_s, new_kb_s, new_vb_s = [], [], [], []
    for l in range(depth):
        modc = mod_all[l][chunk_seq].reshape(n_tok // CHUNK, 1, N_MOD * d)
        h = _norm(x, g_attn3, l, modc, scale1_c, shift1_c)
        proj = _inproj(h, w_in, b_in3, gain_all, flags, l, tn_in)

        bias_ap = _bias_a(rel_bias_a[l], rq_a, band_a + rq_a, True)
        bias_ap = bias_ap.reshape(heads_a // 2, 2 * rq_a, -1, rq_a).transpose(0, 2, 1, 3)
        bias_as = _bias_a(rel_bias_a[l], dec_seq, n_cache_a + dec_seq, False)
        sink_p = jnp.repeat(sinks_b[l], rq_b).reshape(n_kv_pairs, heads_per_pair * rq_b, 1)
        sink_s = jnp.repeat(sinks_b[l], dec_seq).reshape(n_kv_pairs, heads_per_pair * dec_seq, 1)

        oa_p = _attn_a_prompt(proj, bias_ap, batch, seq, width_a, rq_a)
        oa_s = _attn_a_sample(proj, cache_k_a, cache_v_a, bias_as, l, n_p, dec_batch, dec_seq,
                              width_a)
        ob_p = _attn_b_prompt(proj, bias_bp, sink_p, batch, seq, seg[2], seg[3], seg[4],
                              width_b, n_kv_pairs, rq_b)
        ob_s = _attn_b_sample(proj, cache_k_b, cache_v_b, bias_bs, sink_s, l, n_p, dec_batch,
                              dec_seq, seg[2], seg[3], seg[4], width_b, n_kv_pairs)
        attn = jnp.concatenate([jnp.concatenate([oa_p, ob_p], axis=1),
                                jnp.concatenate([oa_s, ob_s], axis=1)], axis=0)
        x = _outproj(attn, w_out, b_out3, x, modc, gate1_c, l)

        hp, e_pad, gate_pad = _router(x, g_ffn3, l, modc, scale2_c, shift2_c, w_router, b_router)
        dest, row_tok, tile_e, n_used = _dispatch_plan(e_pad[:, :TOP_K], n_experts, tm_moe)
        xs = _gather_rows(row_tok, hp, tm_moe)
        act = _expert_up(tile_e, n_used, xs, w_up, b_up, l, tm_moe)
        yb = _expert_down(tile_e, n_used, act, w_down, b_down, l, tm_moe)
        x = _combine(dest, gate_pad, x, modc, gate2_c, yb)

        ka = proj[:, seg[0]:seg[1]]
        va = proj[:, seg[1]:seg[2]]
        kb = proj[:, seg[3]:seg[4]]
        vb = proj[:, seg[4]:seg[5]]
        for dst, src, nh, keep in ((new_ka_p, ka, heads_a, band_a), (new_va_p, va, heads_a, band_a),
                                   (new_kb_p, kb, kv_heads_b, band_b),
                                   (new_vb_p, vb, kv_heads_b, band_b)):
            dst.append(src[:n_p].reshape(batch, seq, nh, HEAD_DIM)[:, -keep:])
        for dst, src, nh in ((new_ka_s, ka, heads_a), (new_va_s, va, heads_a),
                             (new_kb_s, kb, kv_heads_b), (new_vb_s, vb, kv_heads_b)):
            dst.append(src[n_p:].reshape(dec_batch, dec_seq, nh, HEAD_DIM))

    return (x[:n_p].reshape(batch, seq, d), x[n_p:].reshape(dec_batch, dec_seq, d),
            jnp.stack(new_ka_p), jnp.stack(new_va_p), jnp.stack(new_kb_p), jnp.stack(new_vb_p),
            jnp.stack(new_ka_s), jnp.stack(new_va_s), jnp.stack(new_kb_s), jnp.stack(new_vb_s))
```

```python
import functools
import math

import jax
import jax.numpy as jnp
from jax import lax
from jax.experimental import pallas as pl
from jax.experimental.pallas import tpu as pltpu

CHUNK = 64
HEAD_DIM = 64
LANES = 128
MXU_WIDTH = 256
TOP_K = 4
BAND_CHUNKS_A = 8
BAND_CHUNKS_B = 2
REL_CLIP = 128
PAST_LEN = 1024
RMS_EPS = 1e-5
SWIGLU_ALPHA = 1.702
SWIGLU_LIMIT = 7.0
SCALE = HEAD_DIM ** -0.5
NEG = -1e30
N_MOD = 6
VMEM_LIMIT = 56 * 1024 * 1024

F32 = jnp.float32
BF16 = jnp.bfloat16


def _cparams(sem):
    return pltpu.CompilerParams(dimension_semantics=sem, vmem_limit_bytes=VMEM_LIMIT)


def _pick(n, cands):
    for c in cands:
        if n % c == 0:
            return c
    raise ValueError(f"no tile in {cands} divides {n}")


def _nt_dot(a, b):
    return lax.dot_general(a, b, (((1,), (1,)), ((), ())), preferred_element_type=F32)


def _ada_kernel(c_ref, w_ref, b_ref, o_ref):
    c = c_ref[...]
    s = (c * (1.0 / (1.0 + jnp.exp(-c)))).astype(BF16)
    o_ref[...] = jnp.dot(s, w_ref[...].astype(BF16), preferred_element_type=F32) + b_ref[...]


def _ada(c_pad, w_ada, b_ada):
    depth, d, n = w_ada.shape
    rows = c_pad.shape[0]
    tn = _pick(n, (512, 256, 128))
    return pl.pallas_call(
        _ada_kernel,
        grid=(depth, n // tn),
        in_specs=[
            pl.BlockSpec((rows, d), lambda l, j: (0, 0)),
            pl.BlockSpec((None, d, tn), lambda l, j: (l, 0, j)),
            pl.BlockSpec((None, 1, tn), lambda l, j: (l, 0, j)),
        ],
        out_specs=pl.BlockSpec((None, rows, tn), lambda l, j: (l, 0, j)),
        out_shape=jax.ShapeDtypeStruct((depth, rows, n), F32),
        compiler_params=_cparams(("arbitrary", "arbitrary")),
        name="ada_mod",
    )(c_pad, w_ada, b_ada.reshape(depth, 1, n))


def _rms_mod(x, g, scale, shift):
    tm, d = x.shape
    ms = jnp.mean(x * x, axis=-1, keepdims=True)
    y = x * lax.rsqrt(ms + RMS_EPS) * g
    y = y.reshape(tm // CHUNK, CHUNK, d)
    return (y * (1.0 + scale) + shift).reshape(tm, d)


def _norm_kernel(x_ref, g_ref, scale_ref, shift_ref, h_ref):
    h_ref[...] = _rms_mod(x_ref[...], g_ref[...], scale_ref[...], shift_ref[...]).astype(BF16)


def _norm(x, g_all, layer, modc, scale_col, shift_col):
    n, d = x.shape
    tm = _pick(n, (256, 128, 64))
    cpt = tm // CHUNK
    return pl.pallas_call(
        _norm_kernel,
        grid=(n // tm,),
        in_specs=[
            pl.BlockSpec((tm, d), lambda i: (i, 0)),
            pl.BlockSpec((None, 1, d), lambda i: (layer, 0, 0)),
            pl.BlockSpec((cpt, 1, d), lambda i: (i, 0, scale_col)),
            pl.BlockSpec((cpt, 1, d), lambda i: (i, 0, shift_col)),
        ],
        out_specs=pl.BlockSpec((tm, d), lambda i: (i, 0)),
        out_shape=jax.ShapeDtypeStruct((n, d), BF16),
        compiler_params=_cparams(("arbitrary",)),
        name="norm_mod",
    )(x, g_all, modc, modc)


def _inproj_kernel(flag_ref, h_ref, w_ref, b_ref, gain_ref, grp_ref, o_ref, wbf_ref):
    j = pl.program_id(0)

    @pl.when(pl.program_id(1) == 0)
    def _():
        wbf_ref[...] = w_ref[...].astype(BF16)

    y = jnp.dot(h_ref[...], wbf_ref[...], preferred_element_type=F32) + b_ref[...]

    @pl.when(flag_ref[j] == 0)
    def _():
        o_ref[...] = y

    @pl.when(flag_ref[j] != 0)
    def _():
        grp = grp_ref[...]
        gw = grp.shape[0]
        for c in range(y.shape[1] // gw):
            yc = y[:, c * gw:(c + 1) * gw]
            ss = jnp.dot((yc * yc).astype(BF16), grp, preferred_element_type=F32)
            o_ref[:, c * gw:(c + 1) * gw] = (yc * lax.rsqrt(ss * (1.0 / HEAD_DIM) + RMS_EPS)
                                             * gain_ref[:, c * gw:(c + 1) * gw])


def _inproj(h, w_in, b_in, gain, flags, layer, tn):
    n, d = h.shape
    width = w_in.shape[-1]
    tm = _pick(n, (512, 256, 128))
    gw = min(tn, MXU_WIDTH)
    grp = (jnp.arange(gw)[:, None] // HEAD_DIM == jnp.arange(gw)[None, :] // HEAD_DIM).astype(BF16)
    grid_spec = pltpu.PrefetchScalarGridSpec(
        num_scalar_prefetch=1,
        grid=(width // tn, n // tm),
        in_specs=[
            pl.BlockSpec((tm, d), lambda j, i, f: (i, 0)),
            pl.BlockSpec((None, d, tn), lambda j, i, f: (layer, 0, j)),
            pl.BlockSpec((None, 1, tn), lambda j, i, f: (layer, 0, j)),
            pl.BlockSpec((None, 1, tn), lambda j, i, f: (layer, 0, j)),
            pl.BlockSpec((gw, gw), lambda j, i, f: (0, 0)),
        ],
        out_specs=pl.BlockSpec((tm, tn), lambda j, i, f: (i, j)),
        scratch_shapes=[pltpu.VMEM((d, tn), BF16)],
    )
    return pl.pallas_call(
        _inproj_kernel,
        grid_spec=grid_spec,
        out_shape=jax.ShapeDtypeStruct((n, width), F32),
        compiler_params=_cparams(("arbitrary", "arbitrary")),
        name="in_proj",
    )(flags, h, w_in, b_in, gain, grp)


def _softmax_pv(scores, values, sink=None):
    m = None
    for s in scores:
        mj = jnp.max(s, axis=-1, keepdims=True)
        m = mj if m is None else jnp.maximum(m, mj)
    if sink is not None:
        m = jnp.maximum(m, sink)
    den = None
    out = None
    for s, v in zip(scores, values):
        p = jnp.exp(s - m)
        dj = jnp.sum(p, axis=-1, keepdims=True)
        oj = jnp.dot(p.astype(BF16), v, preferred_element_type=F32)
        den = dj if den is None else den + dj
        out = oj if out is None else out + oj
    if sink is not None:
        den = den + jnp.exp(sink - m)
    return out / den


def _stack_pair_a(q):
    lo = lax.broadcasted_iota(jnp.int32, q.shape, 1) < HEAD_DIM
    return jnp.concatenate([jnp.where(lo, q, 0.0), jnp.where(lo, 0.0, q)], axis=0).astype(BF16)


def _unstack_pair_a(o, rows):
    lo = lax.broadcasted_iota(jnp.int32, (rows, LANES), 1) < HEAD_DIM
    return jnp.where(lo, o[:rows], o[rows:])


def _stack_group_b(q, n_heads):
    rows = q.shape[0]
    lo = lax.broadcasted_iota(jnp.int32, (rows, LANES), 1) < HEAD_DIM
    per_kv = n_heads // 2
    parts = []
    for i in range(n_heads):
        blk = q[:, (i // 2) * LANES:(i // 2 + 1) * LANES]
        src_half, kv_half = i % 2, i // per_kv
        if src_half != kv_half:
            blk = pltpu.roll(blk, HEAD_DIM, 1)
        parts.append(jnp.where(lo if kv_half == 0 else jnp.logical_not(lo), blk, 0.0))
    return jnp.concatenate(parts, axis=0).astype(BF16)


def _unstack_group_b(o, rows, n_heads):
    lo = lax.broadcasted_iota(jnp.int32, (rows, LANES), 1) < HEAD_DIM
    per_kv = n_heads // 2
    cols = []
    for mblk in range(n_heads // 2):
        halves = []
        for u in range(2):
            i = 2 * mblk + u
            part = o[i * rows:(i + 1) * rows]
            if i // per_kv != u:
                part = pltpu.roll(part, HEAD_DIM, 1)
            halves.append(part)
        cols.append(jnp.where(lo, halves[0], halves[1]))
    return jnp.concatenate(cols, axis=1)


def _attn_a_prompt_kernel(q_ref, k_ref, v_ref, bias_ref, o_ref, *, rq, n_prev):
    qb = pl.program_id(2)
    qs = _stack_pair_a(q_ref[...] * SCALE)
    scores, values = [], []
    for j in range(n_prev + 1):
        blk = qb - n_prev + j
        start = pl.multiple_of(jnp.maximum(blk, 0) * rq, rq)
        kj = k_ref[pl.ds(start, rq), :].astype(BF16)
        values.append(v_ref[pl.ds(start, rq), :].astype(BF16))
        s = _nt_dot(qs, kj) + bias_ref[j]
        if j < n_prev:
            s = jnp.where(blk >= 0, s, NEG)
        scores.append(s)
    o = _softmax_pv(scores, values)
    o_ref[...] = _unstack_pair_a(o, rq).astype(BF16)


def _attn_a_prompt(proj, bias, batch, seq, width_a, rq):
    n_pairs = width_a // LANES
    n_prev = BAND_CHUNKS_A * CHUNK // rq
    nqb = seq // rq
    kcol, vcol = width_a // LANES, 2 * width_a // LANES
    return pl.pallas_call(
        functools.partial(_attn_a_prompt_kernel, rq=rq, n_prev=n_prev),
        grid=(batch, n_pairs, nqb),
        in_specs=[
            pl.BlockSpec((rq, LANES), lambda b, p, i: (b * nqb + i, p)),
            pl.BlockSpec((seq, LANES), lambda b, p, i: (b, kcol + p)),
            pl.BlockSpec((seq, LANES), lambda b, p, i: (b, vcol + p)),
            pl.BlockSpec((None, n_prev + 1, 2 * rq, rq), lambda b, p, i: (p, 0, 0, 0)),
        ],
        out_specs=pl.BlockSpec((rq, LANES), lambda b, p, i: (b * nqb + i, p)),
        out_shape=jax.ShapeDtypeStruct((batch * seq, width_a), BF16),
        compiler_params=_cparams(("arbitrary", "arbitrary", "arbitrary")),
        name="attn_a_prompt",
    )(proj, proj, proj, bias)


def _attn_a_sample_kernel(q_ref, kn_ref, vn_ref, kc_ref, vc_ref, bias_ref, o_ref, *, n_cache):
    qs = _stack_pair_a(q_ref[...] * SCALE)
    bias = bias_ref[...]
    s_c = _nt_dot(qs, kc_ref[...].astype(BF16)) + bias[:, :n_cache]
    s_n = _nt_dot(qs, kn_ref[...].astype(BF16)) + bias[:, n_cache:]
    o = _softmax_pv([s_c, s_n], [vc_ref[...].astype(BF16), vn_ref[...].astype(BF16)])
    o_ref[...] = _unstack_pair_a(o, q_ref.shape[0]).astype(BF16)


def _attn_a_sample(proj, cache_k, cache_v, bias, layer, row0, dec_batch, dec_seq, width_a):
    n_pairs = width_a // LANES
    n_cache = cache_k.shape[2]
    rb0 = row0 // dec_seq
    kcol, vcol = width_a // LANES, 2 * width_a // LANES
    ck = cache_k.reshape(cache_k.shape[0], dec_batch, n_cache, width_a)
    cv = cache_v.reshape(cache_v.shape[0], dec_batch, n_cache, width_a)
    return pl.pallas_call(
        functools.partial(_attn_a_sample_kernel, n_cache=n_cache),
        grid=(dec_batch, n_pairs),
        in_specs=[
            pl.BlockSpec((dec_seq, LANES), lambda b, p: (rb0 + b, p)),
            pl.BlockSpec((dec_seq, LANES), lambda b, p: (rb0 + b, kcol + p)),
            pl.BlockSpec((dec_seq, LANES), lambda b, p: (rb0 + b, vcol + p)),
            pl.BlockSpec((None, None, n_cache, LANES), lambda b, p: (layer, b, 0, p)),
            pl.BlockSpec((None, None, n_cache, LANES), lambda b, p: (layer, b, 0, p)),
            pl.BlockSpec((None, 2 * dec_seq, n_cache + dec_seq), lambda b, p: (p, 0, 0)),
        ],
        out_specs=pl.BlockSpec((dec_seq, LANES), lambda b, p: (b, p)),
        out_shape=jax.ShapeDtypeStruct((dec_batch * dec_seq, width_a), BF16),
        compiler_params=_cparams(("arbitrary", "arbitrary")),
        name="attn_a_sample",
    )(proj, proj, proj, ck, cv, bias)


def _attn_b_prompt_kernel(q_ref, k_ref, v_ref, bias_ref, sink_ref, o_ref, *, rq, n_heads):
    qb = pl.program_id(2)
    qs = _stack_group_b(q_ref[...] * SCALE, n_heads)
    bias = bias_ref[...]
    prev = qb - 1
    p_start = pl.multiple_of(jnp.maximum(prev, 0) * rq, rq)
    o_start = pl.multiple_of(qb * rq, rq)
    s_p = _nt_dot(qs, k_ref[pl.ds(p_start, rq), :].astype(BF16)) + bias[:, :rq]
    s_p = jnp.where(prev >= 0, s_p, NEG)
    s_o = _nt_dot(qs, k_ref[pl.ds(o_start, rq), :].astype(BF16)) + bias[:, rq:]
    o = _softmax_pv([s_p, s_o],
                    [v_ref[pl.ds(p_start, rq), :].astype(BF16),
                     v_ref[pl.ds(o_start, rq), :].astype(BF16)],
                    sink_ref[...])
    o_ref[...] = _unstack_group_b(o, rq, n_heads).astype(BF16)


def _attn_b_prompt(proj, bias, sink, batch, seq, qcol0, kcol0, vcol0, width_b, n_kv_pairs, rq):
    n_heads = width_b // HEAD_DIM // n_kv_pairs
    qw = n_heads * HEAD_DIM
    nqb = seq // rq
    qc, kc, vc = qcol0 // qw, kcol0 // LANES, vcol0 // LANES
    return pl.pallas_call(
        functools.partial(_attn_b_prompt_kernel, rq=rq, n_heads=n_heads),
        grid=(batch, n_kv_pairs, nqb),
        in_specs=[
            pl.BlockSpec((rq, qw), lambda b, p, i: (b * nqb + i, qc + p)),
            pl.BlockSpec((seq, LANES), lambda b, p, i: (b, kc + p)),
            pl.BlockSpec((seq, LANES), lambda b, p, i: (b, vc + p)),
            pl.BlockSpec((None, n_heads * rq, 2 * rq), lambda b, p, i: (p, 0, 0)),
            pl.BlockSpec((None, n_heads * rq, 1), lambda b, p, i: (p, 0, 0)),
        ],
        out_specs=pl.BlockSpec((rq, qw), lambda b, p, i: (b * nqb + i, p)),
        out_shape=jax.ShapeDtypeStruct((batch * seq, width_b), BF16),
        compiler_params=_cparams(("arbitrary", "arbitrary", "arbitrary")),
        name="attn_b_prompt",
    )(proj, proj, proj, bias, sink)


def _attn_b_sample_kernel(q_ref, kn_ref, vn_ref, kc_ref, vc_ref, bias_ref, sink_ref, o_ref, *,
                          n_cache, n_heads):
    qs = _stack_group_b(q_ref[...] * SCALE, n_heads)
    bias = bias_ref[...]
    s_c = _nt_dot(qs, kc_ref[...].astype(BF16)) + bias[:, :n_cache]
    s_n = _nt_dot(qs, kn_ref[...].astype(BF16)) + bias[:, n_cache:]
    o = _softmax_pv([s_c, s_n], [vc_ref[...].astype(BF16), vn_ref[...].astype(BF16)],
                    sink_ref[...])
    o_ref[...] = _unstack_group_b(o, q_ref.shape[0], n_heads).astype(BF16)


def _attn_b_sample(proj, cache_k, cache_v, bias, sink, layer, row0, dec_batch, dec_seq,
                   qcol0, kcol0, vcol0, width_b, n_kv_pairs):
    n_heads = width_b // HEAD_DIM // n_kv_pairs
    qw = n_heads * HEAD_DIM
    n_cache = cache_k.shape[2]
    kvw = cache_k.shape[3] * cache_k.shape[4]
    rb0 = row0 // dec_seq
    qc, kc, vc = qcol0 // qw, kcol0 // LANES, vcol0 // LANES
    ck = cache_k.reshape(cache_k.shape[0], dec_batch, n_cache, kvw)
    cv = cache_v.reshape(cache_v.shape[0], dec_batch, n_cache, kvw)
    return pl.pallas_call(
        functools.partial(_attn_b_sample_kernel, n_cache=n_cache, n_heads=n_heads),
        grid=(dec_batch, n_kv_pairs),
        in_specs=[
            pl.BlockSpec((dec_seq, qw), lambda b, p: (rb0 + b, qc + p)),
            pl.BlockSpec((dec_seq, LANES), lambda b, p: (rb0 + b, kc + p)),
            pl.BlockSpec((dec_seq, LANES), lambda b, p: (rb0 + b, vc + p)),
            pl.BlockSpec((None, None, n_cache, LANES), lambda b, p: (layer, b, 0, p)),
            pl.BlockSpec((None, None, n_cache, LANES), lambda b, p: (layer, b, 0, p)),
            pl.BlockSpec((None, n_heads * dec_seq, n_cache + dec_seq), lambda b, p: (p, 0, 0)),
            pl.BlockSpec((None, n_heads * dec_seq, 1), lambda b, p: (p, 0, 0)),
        ],
        out_specs=pl.BlockSpec((dec_seq, qw), lambda b, p: (b, p)),
        out_shape=jax.ShapeDtypeStruct((dec_batch * dec_seq, width_b), BF16),
        compiler_params=_cparams(("arbitrary", "arbitrary")),
        name="attn_b_sample",
    )(proj, proj, proj, ck, cv, bias, sink)


def _outproj_kernel(a_ref, w_ref, b_ref, x_ref, gate_ref, o_ref, wbf_ref):
    @pl.when(pl.program_id(1) == 0)
    def _():
        wbf_ref[...] = w_ref[...].astype(BF16)

    y = jnp.dot(a_ref[...], wbf_ref[...], preferred_element_type=F32) + b_ref[...]
    tm, tn = y.shape
    y = y.reshape(tm // CHUNK, CHUNK, tn) * gate_ref[...]
    o_ref[...] = x_ref[...] + y.reshape(tm, tn)


def _outproj(attn, w_out, b_out, x, modc, gate_col, layer):
    n, kdim = attn.shape
    d = w_out.shape[-1]
    tm = _pick(n, (512, 256, 128))
    tn = _pick(d, (512, 256, 128))
    cpt = tm // CHUNK
    gc = gate_col * (d // tn)
    return pl.pallas_call(
        _outproj_kernel,
        grid=(d // tn, n // tm),
        in_specs=[
            pl.BlockSpec((tm, kdim), lambda j, i: (i, 0)),
            pl.BlockSpec((None, kdim, tn), lambda j, i: (layer, 0, j)),
            pl.BlockSpec((None, 1, tn), lambda j, i: (layer, 0, j)),
            pl.BlockSpec((tm, tn), lambda j, i: (i, j)),
            pl.BlockSpec((cpt, 1, tn), lambda j, i: (i, 0, gc + j)),
        ],
        out_specs=pl.BlockSpec((tm, tn), lambda j, i: (i, j)),
        out_shape=jax.ShapeDtypeStruct((n, d), F32),
        scratch_shapes=[pltpu.VMEM((kdim, tn), BF16)],
        compiler_params=_cparams(("arbitrary", "arbitrary")),
        name="out_proj",
    )(attn, w_out, b_out, x, modc)


def _router_kernel(x_ref, g_ref, scale_ref, shift_ref, wr_ref, br_ref,
                   hp_ref, e_ref, gate_ref, whi_ref, wlo_ref, *, n_experts):
    @pl.when(pl.program_id(0) == 0)
    def _():
        w = wr_ref[...]
        hi = w.astype(BF16)
        whi_ref[...] = hi
        wlo_ref[...] = (w - hi.astype(F32)).astype(BF16)

    h = _rms_mod(x_ref[...], g_ref[...], scale_ref[...], shift_ref[...])
    tm, d = h.shape
    hi = h.astype(BF16)
    hi32 = hi.astype(F32)
    lo = (h - hi32).astype(BF16)
    bits = lax.bitcast_convert_type(hi32, jnp.uint32)
    hp_ref[...] = bits[:, :d // 2] | (bits[:, d // 2:] >> 16)

    logits = (jnp.dot(hi, whi_ref[...], preferred_element_type=F32)
              + jnp.dot(lo, whi_ref[...], preferred_element_type=F32)
              + jnp.dot(hi, wlo_ref[...], preferred_element_type=F32)) + br_ref[...]

    lane_e = lax.broadcasted_iota(jnp.int32, (tm, n_experts), 1).astype(F32)
    lane_o = lax.broadcasted_iota(jnp.int32, (tm, LANES), 1)
    work = logits
    tops, idxs = [], []
    for _ in range(TOP_K):
        mx = jnp.max(work, axis=-1, keepdims=True)
        ix = jnp.min(jnp.where(work == mx, lane_e, float(n_experts)), axis=-1, keepdims=True)
        tops.append(mx)
        idxs.append(ix)
        work = jnp.where(lane_e == ix, -jnp.inf, work)
    exps = [jnp.exp(t - tops[0]) for t in tops]
    den = exps[0] + exps[1] + exps[2] + exps[3]
    e_out = jnp.zeros((tm, LANES), F32)
    g_out = jnp.zeros((tm, LANES), F32)
    for k in range(TOP_K):
        e_out = jnp.where(lane_o == k, idxs[k], e_out)
        g_out = jnp.where(lane_o == k, exps[k] / den, g_out)
    e_ref[...] = e_out.astype(jnp.int32)
    gate_ref[...] = g_out


def _router(x, g_all, layer, modc, scale_col, shift_col, w_router, b_router):
    n, d = x.shape
    n_experts = w_router.shape[-1]
    tm = _pick(n, (256, 128, 64))
    cpt = tm // CHUNK
    depth = w_router.shape[0]
    return pl.pallas_call(
        functools.partial(_router_kernel, n_experts=n_experts),
        grid=(n // tm,),
        in_specs=[
            pl.BlockSpec((tm, d), lambda i: (i, 0)),
            pl.BlockSpec((None, 1, d), lambda i: (layer, 0, 0)),
            pl.BlockSpec((cpt, 1, d), lambda i: (i, 0, scale_col)),
            pl.BlockSpec((cpt, 1, d), lambda i: (i, 0, shift_col)),
            pl.BlockSpec((None, d, n_experts), lambda i: (layer, 0, 0)),
            pl.BlockSpec((None, 1, n_experts), lambda i: (layer, 0, 0)),
        ],
        out_specs=[
            pl.BlockSpec((tm, d // 2), lambda i: (i, 0)),
            pl.BlockSpec((tm, LANES), lambda i: (i, 0)),
            pl.BlockSpec((tm, LANES), lambda i: (i, 0)),
        ],
        out_shape=[
            jax.ShapeDtypeStruct((n, d // 2), jnp.uint32),
            jax.ShapeDtypeStruct((n, LANES), jnp.int32),
            jax.ShapeDtypeStruct((n, LANES), F32),
        ],
        scratch_shapes=[pltpu.VMEM((d, n_experts), BF16), pltpu.VMEM((d, n_experts), BF16)],
        compiler_params=_cparams(("arbitrary",)),
        name="norm_router",
    )(x, g_all, modc, modc, w_router, b_router.reshape(depth, 1, n_experts))


DMA_ISSUE_UNROLL = 8


def _row_copy(src, dst, sem, src_row, dst_row):
    return pltpu.make_async_copy(src.at[pl.ds(src_row, 1)], dst.at[pl.ds(dst_row, 1)], sem)


def _tile_wait(src, dst, sem, tm):
    pltpu.make_async_copy(src.at[pl.ds(0, tm)], dst.at[pl.ds(0, tm)], sem).wait()


def _gather_kernel(tok_ref, h_hbm, o_hbm, sem, *, tm):
    i = pl.program_id(0)
    base = i * tm

    def issue(r, carry):
        _row_copy(h_hbm, o_hbm, sem, tok_ref[base + r], base + r).start()
        return carry

    lax.fori_loop(0, tm, issue, 0, unroll=DMA_ISSUE_UNROLL)

    @pl.when(i > 0)
    def _():
        _tile_wait(h_hbm, o_hbm, sem, tm)

    @pl.when(i == pl.num_programs(0) - 1)
    def _():
        _tile_wait(h_hbm, o_hbm, sem, tm)


def _gather_rows(row_tok, hp, tm):
    n_rows = row_tok.shape[0]
    grid_spec = pltpu.PrefetchScalarGridSpec(
        num_scalar_prefetch=1,
        grid=(n_rows // tm,),
        in_specs=[pl.BlockSpec(memory_space=pl.ANY)],
        out_specs=pl.BlockSpec(memory_space=pl.ANY),
        scratch_shapes=[pltpu.SemaphoreType.DMA],
    )
    return pl.pallas_call(
        functools.partial(_gather_kernel, tm=tm),
        grid_spec=grid_spec,
        out_shape=jax.ShapeDtypeStruct((n_rows, hp.shape[1]), hp.dtype),
        compiler_params=_cparams(("arbitrary",)),
        name="moe_gather",
    )(row_tok, hp)


def _unpack_rows(words):
    first = lax.bitcast_convert_type(words & jnp.uint32(0xFFFF0000), F32).astype(BF16)
    second = lax.bitcast_convert_type(words << 16, F32).astype(BF16)
    return first, second


def _expert_changed(te_ref, i):
    return jnp.logical_or(i == 0, te_ref[i] != te_ref[jnp.maximum(i - 1, 0)])


def _up_kernel(te_ref, nu_ref, x_ref, wg_ref, wl_ref, bg_ref, bl_ref, o_ref, wgb_ref, wlb_ref):
    i = pl.program_id(1)

    @pl.when(_expert_changed(te_ref, i))
    def _():
        wgb_ref[...] = wg_ref[...].astype(BF16)
        wlb_ref[...] = wl_ref[...].astype(BF16)

    @pl.when(i < nu_ref[0])
    def _():
        x0, x1 = _unpack_rows(x_ref[...])
        half = x0.shape[1]
        glu = (jnp.dot(x0, wgb_ref[:half, :], preferred_element_type=F32)
               + jnp.dot(x1, wgb_ref[half:, :], preferred_element_type=F32)) + bg_ref[...]
        lin = (jnp.dot(x0, wlb_ref[:half, :], preferred_element_type=F32)
               + jnp.dot(x1, wlb_ref[half:, :], preferred_element_type=F32)) + bl_ref[...]
        glu = jnp.minimum(glu, SWIGLU_LIMIT)
        lin = jnp.clip(lin, -SWIGLU_LIMIT, SWIGLU_LIMIT)
        act = glu * (1.0 / (1.0 + jnp.exp(-SWIGLU_ALPHA * glu))) * (lin + 1.0)
        o_ref[...] = act.astype(BF16)

    @pl.when(i >= nu_ref[0])
    def _():
        o_ref[...] = jnp.zeros_like(o_ref)


def _expert_up(tile_e, n_used, xs, w_up, b_up, layer, tm):
    depth, n_experts, d, two_ff = w_up.shape
    n_rows, half = xs.shape
    d_ff = two_ff // 2
    tn = _pick(d_ff, (512, 256, 128))
    nj = d_ff // tn
    b4 = b_up.reshape(depth, n_experts, 1, two_ff)
    grid_spec = pltpu.PrefetchScalarGridSpec(
        num_scalar_prefetch=2,
        grid=(nj, n_rows // tm),
        in_specs=[
            pl.BlockSpec((tm, half),
                         lambda j, i, te, nu: (jnp.minimum(i, jnp.maximum(nu[0] - 1, 0)), 0)),
            pl.BlockSpec((None, None, d, tn), lambda j, i, te, nu: (layer, te[i], 0, j)),
            pl.BlockSpec((None, None, d, tn), lambda j, i, te, nu: (layer, te[i], 0, nj + j)),
            pl.BlockSpec((None, None, 1, tn), lambda j, i, te, nu: (layer, te[i], 0, j)),
            pl.BlockSpec((None, None, 1, tn), lambda j, i, te, nu: (layer, te[i], 0, nj + j)),
        ],
        out_specs=pl.BlockSpec((tm, tn), lambda j, i, te, nu: (i, j)),
        scratch_shapes=[pltpu.VMEM((d, tn), BF16), pltpu.VMEM((d, tn), BF16)],
    )
    return pl.pallas_call(
        _up_kernel,
        grid_spec=grid_spec,
        out_shape=jax.ShapeDtypeStruct((n_rows, d_ff), BF16),
        compiler_params=_cparams(("arbitrary", "arbitrary")),
        name="expert_up",
    )(tile_e, n_used, xs, w_up, w_up, b4, b4)


def _down_kernel(te_ref, nu_ref, a_ref, w_ref, b_ref, o_ref, wb_ref):
    i = pl.program_id(1)

    @pl.when(_expert_changed(te_ref, i))
    def _():
        wb_ref[...] = w_ref[...].astype(BF16)

    @pl.when(i < nu_ref[0])
    def _():
        o_ref[...] = jnp.dot(a_ref[...], wb_ref[...], preferred_element_type=F32) + b_ref[...]

    @pl.when(i >= nu_ref[0])
    def _():
        o_ref[...] = jnp.zeros_like(o_ref)


def _expert_down(tile_e, n_used, act, w_down, b_down, layer, tm):
    n_rows, d_ff = act.shape
    depth, n_experts, _, d = w_down.shape
    tn = _pick(d, (1024, 512, 256, 128))
    b4 = b_down.reshape(depth, n_experts, 1, d)
    grid_spec = pltpu.PrefetchScalarGridSpec(
        num_scalar_prefetch=2,
        grid=(d // tn, n_rows // tm),
        in_specs=[
            pl.BlockSpec((tm, d_ff), lambda j, i, te, nu: (i, 0)),
            pl.BlockSpec((None, None, d_ff, tn), lambda j, i, te, nu: (layer, te[i], 0, j)),
            pl.BlockSpec((None, None, 1, tn), lambda j, i, te, nu: (layer, te[i], 0, j)),
        ],
        out_specs=pl.BlockSpec((tm, tn), lambda j, i, te, nu: (i, j)),
        scratch_shapes=[pltpu.VMEM((d_ff, tn), BF16)],
    )
    return pl.pallas_call(
        _down_kernel,
        grid_spec=grid_spec,
        out_shape=jax.ShapeDtypeStruct((n_rows, d), F32),
        compiler_params=_cparams(("arbitrary", "arbitrary")),
        name="expert_down",
    )(tile_e, n_used, act, w_down, b4)


def _combine_kernel(pos_ref, gate_ref, x_ref, g2_ref, y_hbm, o_ref, buf_ref, sem, *, tm):
    i = pl.program_id(0)
    n_steps = pl.num_programs(0)

    def issue_tile(tile, slot):
        base = tile * (tm * TOP_K)

        def issue(r, carry):
            for k in range(TOP_K):
                _row_copy(y_hbm, buf_ref.at[slot, k], sem.at[slot],
                          pos_ref[base + r * TOP_K + k], r).start()
            return carry

        lax.fori_loop(0, tm, issue, 0, unroll=DMA_ISSUE_UNROLL // TOP_K)

    slot = i % 2

    @pl.when(i == 0)
    def _():
        issue_tile(0, 0)

    @pl.when(i + 1 < n_steps)
    def _():
        issue_tile(i + 1, 1 - slot)

    for k in range(TOP_K):
        _tile_wait(y_hbm, buf_ref.at[slot, k], sem.at[slot], tm)

    gate = gate_ref[...]
    acc = gate[:, 0:1] * buf_ref[slot, 0]
    for k in range(1, TOP_K):
        acc = acc + gate[:, k:k + 1] * buf_ref[slot, k]
    d = acc.shape[1]
    acc = acc.reshape(tm // CHUNK, CHUNK, d) * g2_ref[...]
    o_ref[...] = x_ref[...] + acc.reshape(tm, d)


def _combine(pos, gate, x, modc, gate_col, yb):
    n, d = x.shape
    tm = _pick(n, (128, 64))
    cpt = tm // CHUNK
    grid_spec = pltpu.PrefetchScalarGridSpec(
        num_scalar_prefetch=1,
        grid=(n // tm,),
        in_specs=[
            pl.BlockSpec((tm, LANES), lambda i, p: (i, 0)),
            pl.BlockSpec((tm, d), lambda i, p: (i, 0)),
            pl.BlockSpec((cpt, 1, d), lambda i, p: (i, 0, gate_col)),
            pl.BlockSpec(memory_space=pl.ANY),
        ],
        out_specs=pl.BlockSpec((tm, d), lambda i, p: (i, 0)),
        scratch_shapes=[pltpu.VMEM((2, TOP_K, tm, d), F32), pltpu.SemaphoreType.DMA((2,))],
    )
    return pl.pallas_call(
        functools.partial(_combine_kernel, tm=tm),
        grid_spec=grid_spec,
        out_shape=jax.ShapeDtypeStruct((n, d), F32),
        compiler_params=_cparams(("arbitrary",)),
        name="moe_combine",
    )(pos, gate, x, modc, yb)


def _dispatch_plan(top_e, n_experts, tm):
    n_tok = top_e.shape[0]
    n_assign = n_tok * TOP_K
    flat_e = top_e.reshape(n_assign)
    onehot = (flat_e[:, None] == jnp.arange(n_experts, dtype=jnp.int32)[None, :]).astype(jnp.int32)
    csum = jnp.cumsum(onehot, axis=0)
    counts = csum[-1]
    rank = jnp.take_along_axis(csum, flat_e[:, None], axis=1)[:, 0] - 1
    padded = (counts + tm - 1) // tm * tm
    p_end = jnp.cumsum(padded)
    p_start = p_end - padded
    dest = (p_start[flat_e] + rank).astype(jnp.int32)
    n_tiles = n_assign // tm + n_experts
    row_tok = jnp.zeros((n_tiles * tm,), jnp.int32).at[dest].set(
        jnp.arange(n_assign, dtype=jnp.int32) // TOP_K)
    tile_start = jnp.arange(n_tiles, dtype=jnp.int32) * tm
    tile_e = jnp.minimum(jnp.searchsorted(p_end, tile_start, side="right"),
                         n_experts - 1).astype(jnp.int32)
    n_used = (p_end[-1] // tm).astype(jnp.int32).reshape(1)
    return dest, row_tok, tile_e, n_used


def _bias_a(rel_bias, rq, n_keys, masked):
    n_heads = rel_bias.shape[0]
    qi = jnp.arange(rq)[:, None]
    ki = jnp.arange(n_keys)[None, :]
    off = n_keys - rq
    n_diag = n_keys + rq - 1
    rel = jnp.clip(jnp.arange(n_diag) - (rq - 1) - off, -REL_CLIP, REL_CLIP) + REL_CLIP
    ext = jnp.pad(jnp.take(rel_bias, rel, axis=1), ((0, 0), (0, 1)))
    skew = jnp.tile(ext, (1, rq))[:, :rq * n_diag].reshape(n_heads, rq, n_diag)
    bias = skew[:, :, rq - 1:rq - 1 + n_keys]
    if masked:
        qc = (qi + off) // CHUNK
        kc = ki // CHUNK
        ok = (kc <= qc) & (kc >= qc - BAND_CHUNKS_A)
        bias = jnp.where(ok[None], bias, NEG)
    return bias.reshape(n_heads // 2, 2 * rq, n_keys)


def _bias_b(n_heads_total, n_kv_pairs, rq, n_keys, masked):
    slopes = 2.0 ** (-8.0 * jnp.arange(1, n_heads_total + 1, dtype=F32) / n_heads_total)
    qi = jnp.arange(rq)[:, None]
    ki = jnp.arange(n_keys)[None, :]
    off = n_keys - rq
    dist = jnp.abs(qi - (ki - off)).astype(F32)
    bias = -slopes[:, None, None] * dist[None]
    if masked:
        qc = (qi + off) // CHUNK
        kc = ki // CHUNK
        ok = (kc <= qc) & (kc >= qc - BAND_CHUNKS_B)
        bias = jnp.where(ok[None], bias, NEG)
    return bias.reshape(n_kv_pairs, (n_heads_total // n_kv_pairs) * rq, n_keys)


def kernel(x_prompt, x_sample, c_prompt, c_sample, cache_k_a, cache_v_a, cache_k_b, cache_v_b,
           w_ada, b_ada, g_attn, g_ffn, w_in, b_in, g_q_a, g_k_a, g_q_b, g_k_b, rel_bias_a,
           sinks_b, w_out, b_out, w_router, b_router, w_up, b_up, w_down, b_down):
    batch, seq, d = x_prompt.shape
    dec_batch, dec_seq, _ = x_sample.shape
    depth = w_in.shape[0]
    heads_a = rel_bias_a.shape[1]
    heads_b = sinks_b.shape[1]
    kv_heads_b = cache_k_b.shape[3]
    n_experts = w_router.shape[-1]
    width_a, width_b, kv_width_b = heads_a * HEAD_DIM, heads_b * HEAD_DIM, kv_heads_b * HEAD_DIM
    in_width = w_in.shape[-1]
    n_cache_a, n_cache_b = cache_k_a.shape[2], cache_k_b.shape[2]
    band_a, band_b = BAND_CHUNKS_A * CHUNK, BAND_CHUNKS_B * CHUNK
    assert dec_seq == CHUNK and PAST_LEN % CHUNK == 0 and seq % CHUNK == 0
    assert n_cache_a == band_a and n_cache_b == band_b
    assert in_width == 3 * width_a + width_b + 2 * kv_width_b and width_a + width_b == d
    assert kv_heads_b % 2 == 0 and heads_a % 2 == 0

    n_p, n_s = batch * seq, dec_batch * dec_seq
    n_tok = n_p + n_s
    x = jnp.concatenate([x_prompt.reshape(n_p, d), x_sample.reshape(n_s, d)], axis=0)

    n_seq = batch + dec_batch
    c_all = jnp.concatenate([c_prompt, c_sample], axis=0)
    c_pad = jnp.pad(c_all, ((0, -n_seq % 8), (0, 0)))
    mod_all = _ada(c_pad, w_ada, b_ada)
    shift1_c, scale1_c, gate1_c, shift2_c, scale2_c, gate2_c = range(N_MOD)

    seg = (width_a, 2 * width_a, 3 * width_a, 3 * width_a + width_b,
           3 * width_a + width_b + kv_width_b, in_width)
    tn_in = _pick(math.gcd(*seg), (512, 256, 128))
    col = jnp.arange(in_width)
    is_v = ((col >= seg[1]) & (col < seg[2])) | (col >= seg[4])
    flags = jnp.logical_not(is_v[::tn_in]).astype(jnp.int32)

    def gains(l):
        ones_a, ones_kv = jnp.ones((width_a,), F32), jnp.ones((kv_width_b,), F32)
        return jnp.concatenate([jnp.tile(g_q_a[l], heads_a), jnp.tile(g_k_a[l], heads_a), ones_a,
                                jnp.tile(g_q_b[l], heads_b), jnp.tile(g_k_b[l], kv_heads_b),
                                ones_kv])

    gain_all = jnp.stack([gains(l) for l in range(depth)]).reshape(depth, 1, in_width)
    b_in3 = b_in.reshape(depth, 1, in_width)
    b_out3 = b_out.reshape(depth, 1, d)
    g_attn3 = g_attn.reshape(depth, 1, d)
    g_ffn3 = g_ffn.reshape(depth, 1, d)

    rq_a = 4 * CHUNK
    rq_b = 2 * CHUNK
    n_kv_pairs = kv_heads_b // 2
    heads_per_pair = heads_b // n_kv_pairs
    bias_bp = _bias_b(heads_b, n_kv_pairs, rq_b, 2 * rq_b, True)
    bias_bs = _bias_b(heads_b, n_kv_pairs, dec_seq, n_cache_b + dec_seq, False)

    tm_moe = 256
    new_ka_p, new_va_p, new_kb_p, new_vb_p = [], [], [], []
    new_ka_s, new_va_s, new_kb_s, new_vb_s = [], [], [], []
    for l in range(depth):
        mod_p = jnp.broadcast_to(mod_all[l, :batch, None, :], (batch, seq // CHUNK, N_MOD * d))
        modc = jnp.concatenate([mod_p.reshape(n_p // CHUNK, N_MOD * d), mod_all[l, batch:n_seq]],
                               axis=0).reshape(n_tok // CHUNK, 1, N_MOD * d)
        h = _norm(x, g_attn3, l, modc, scale1_c, shift1_c)
        proj = _inproj(h, w_in, b_in3, gain_all, flags, l, tn_in)

        bias_ap = _bias_a(rel_bias_a[l], rq_a, band_a + rq_a, True)
        bias_ap = bias_ap.reshape(heads_a // 2, 2 * rq_a, -1, rq_a).transpose(0, 2, 1, 3)
        bias_as = _bias_a(rel_bias_a[l], dec_seq, n_cache_a + dec_seq, False)
        sink_p = jnp.repeat(sinks_b[l], rq_b).reshape(n_kv_pairs, heads_per_pair * rq_b, 1)
        sink_s = jnp.repeat(sinks_b[l], dec_seq).reshape(n_kv_pairs, heads_per_pair * dec_seq, 1)

        oa_p = _attn_a_prompt(proj, bias_ap, batch, seq, width_a, rq_a)
        oa_s = _attn_a_sample(proj, cache_k_a, cache_v_a, bias_as, l, n_p, dec_batch, dec_seq,
                              width_a)
        ob_p = _attn_b_prompt(proj, bias_bp, sink_p, batch, seq, seg[2], seg[3], seg[4],
                              width_b, n_kv_pairs, rq_b)
        ob_s = _attn_b_sample(proj, cache_k_b, cache_v_b, bias_bs, sink_s, l, n_p, dec_batch,
                              dec_seq, seg[2], seg[3], seg[4], width_b, n_kv_pairs)
        attn = jnp.concatenate([jnp.concatenate([oa_p, ob_p], axis=1),
                                jnp.concatenate([oa_s, ob_s], axis=1)], axis=0)
        x = _outproj(attn, w_out, b_out3, x, modc, gate1_c, l)

        hp, e_pad, gate_pad = _router(x, g_ffn3, l, modc, scale2_c, shift2_c, w_router, b_router)
        dest, row_tok, tile_e, n_used = _dispatch_plan(e_pad[:, :TOP_K], n_experts, tm_moe)
        xs = _gather_rows(row_tok, hp, tm_moe)
        act = _expert_up(tile_e, n_used, xs, w_up, b_up, l, tm_moe)
        yb = _expert_down(tile_e, n_used, act, w_down, b_down, l, tm_moe)
        x = _combine(dest, gate_pad, x, modc, gate2_c, yb)

        ka = proj[:, seg[0]:seg[1]]
        va = proj[:, seg[1]:seg[2]]
        kb = proj[:, seg[3]:seg[4]]
        vb = proj[:, seg[4]:seg[5]]
        for dst, src, nh, keep in ((new_ka_p, ka, heads_a, band_a), (new_va_p, va, heads_a, band_a),
                                   (new_kb_p, kb, kv_heads_b, band_b),
                                   (new_vb_p, vb, kv_heads_b, band_b)):
            dst.append(src[:n_p].reshape(batch, seq, nh, HEAD_DIM)[:, -keep:])
        for dst, src, nh in ((new_ka_s, ka, heads_a), (new_va_s, va, heads_a),
                             (new_kb_s, kb, kv_heads_b), (new_vb_s, vb, kv_heads_b)):
            dst.append(src[n_p:].reshape(dec_batch, dec_seq, nh, HEAD_DIM))

    return (x[:n_p].reshape(batch, seq, d), x[n_p:].reshape(dec_batch, dec_seq, d),
            jnp.stack(new_ka_p), jnp.stack(new_va_p), jnp.stack(new_kb_p), jnp.stack(new_vb_p),
            jnp.stack(new_ka_s), jnp.stack(new_va_s), jnp.stack(new_kb_s), jnp.stack(new_vb_s))
```

```python
import functools
import math

import jax
import jax.numpy as jnp
from jax import lax
from jax.experimental import pallas as pl
from jax.experimental.pallas import tpu as pltpu

CHUNK = 64
HEAD_DIM = 64
LANES = 128
MXU_WIDTH = 256
TOP_K = 4
BAND_CHUNKS_A = 8
BAND_CHUNKS_B = 2
REL_CLIP = 128
PAST_LEN = 1024
RMS_EPS = 1e-5
SWIGLU_ALPHA = 1.702
SWIGLU_LIMIT = 7.0
SCALE = HEAD_DIM ** -0.5
NEG = -1e30
N_MOD = 6
VMEM_LIMIT = 56 * 1024 * 1024

F32 = jnp.float32
BF16 = jnp.bfloat16


def _cparams(sem):
    return pltpu.CompilerParams(dimension_semantics=sem, vmem_limit_bytes=VMEM_LIMIT)


def _pick(n, cands):
    for c in cands:
        if n % c == 0:
            return c
    raise ValueError(f"no tile in {cands} divides {n}")


def _nt_dot(a, b):
    return lax.dot_general(a, b, (((1,), (1,)), ((), ())), preferred_element_type=F32)


def _ada_kernel(c_ref, w_ref, b_ref, o_ref):
    c = c_ref[...]
    s = (c * (1.0 / (1.0 + jnp.exp(-c)))).astype(BF16)
    o_ref[...] = jnp.dot(s, w_ref[...].astype(BF16), preferred_element_type=F32) + b_ref[...]


def _ada(c_pad, w_ada, b_ada):
    depth, d, n = w_ada.shape
    rows = c_pad.shape[0]
    tn = _pick(n, (512, 256, 128))
    return pl.pallas_call(
        _ada_kernel,
        grid=(depth, n // tn),
        in_specs=[
            pl.BlockSpec((rows, d), lambda l, j: (0, 0)),
            pl.BlockSpec((None, d, tn), lambda l, j: (l, 0, j)),
            pl.BlockSpec((None, 1, tn), lambda l, j: (l, 0, j)),
        ],
        out_specs=pl.BlockSpec((None, rows, tn), lambda l, j: (l, 0, j)),
        out_shape=jax.ShapeDtypeStruct((depth, rows, n), F32),
        compiler_params=_cparams(("arbitrary", "arbitrary")),
        name="ada_mod",
    )(c_pad, w_ada, b_ada.reshape(depth, 1, n))


def _rms_mod(x, g, scale, shift):
    tm, d = x.shape
    ms = jnp.mean(x * x, axis=-1, keepdims=True)
    y = x * lax.rsqrt(ms + RMS_EPS) * g
    y = y.reshape(tm // CHUNK, CHUNK, d)
    return (y * (1.0 + scale) + shift).reshape(tm, d)


def _norm_kernel(x_ref, g_ref, scale_ref, shift_ref, h_ref):
    h_ref[...] = _rms_mod(x_ref[...], g_ref[...], scale_ref[...], shift_ref[...]).astype(BF16)


def _norm(x, g_all, layer, modc, scale_col, shift_col):
    n, d = x.shape
    tm = _pick(n, (256, 128, 64))
    cpt = tm // CHUNK
    return pl.pallas_call(
        _norm_kernel,
        grid=(n // tm,),
        in_specs=[
            pl.BlockSpec((tm, d), lambda i: (i, 0)),
            pl.BlockSpec((None, 1, d), lambda i: (layer, 0, 0)),
            pl.BlockSpec((cpt, 1, d), lambda i: (i, 0, scale_col)),
            pl.BlockSpec((cpt, 1, d), lambda i: (i, 0, shift_col)),
        ],
        out_specs=pl.BlockSpec((tm, d), lambda i: (i, 0)),
        out_shape=jax.ShapeDtypeStruct((n, d), BF16),
        compiler_params=_cparams(("arbitrary",)),
        name="norm_mod",
    )(x, g_all, modc, modc)


def _inproj_kernel(flag_ref, h_ref, w_ref, b_ref, gain_ref, grp_ref, o_ref, wbf_ref):
    j = pl.program_id(0)

    @pl.when(pl.program_id(1) == 0)
    def _():
        wbf_ref[...] = w_ref[...].astype(BF16)

    y = jnp.dot(h_ref[...], wbf_ref[...], preferred_element_type=F32) + b_ref[...]

    @pl.when(flag_ref[j] == 0)
    def _():
        o_ref[...] = y

    @pl.when(flag_ref[j] != 0)
    def _():
        grp = grp_ref[...]
        gw = grp.shape[0]
        for c in range(y.shape[1] // gw):
            yc = y[:, c * gw:(c + 1) * gw]
            ss = jnp.dot((yc * yc).astype(BF16), grp, preferred_element_type=F32)
            o_ref[:, c * gw:(c + 1) * gw] = (yc * lax.rsqrt(ss * (1.0 / HEAD_DIM) + RMS_EPS)
                                             * gain_ref[:, c * gw:(c + 1) * gw])


def _inproj(h, w_in, b_in, gain, flags, layer, tn):
    n, d = h.shape
    width = w_in.shape[-1]
    tm = _pick(n, (512, 256, 128))
    gw = min(tn, MXU_WIDTH)
    grp = (jnp.arange(gw)[:, None] // HEAD_DIM == jnp.arange(gw)[None, :] // HEAD_DIM).astype(BF16)
    grid_spec = pltpu.PrefetchScalarGridSpec(
        num_scalar_prefetch=1,
        grid=(width // tn, n // tm),
        in_specs=[
            pl.BlockSpec((tm, d), lambda j, i, f: (i, 0)),
            pl.BlockSpec((None, d, tn), lambda j, i, f: (layer, 0, j)),
            pl.BlockSpec((None, 1, tn), lambda j, i, f: (layer, 0, j)),
            pl.BlockSpec((None, 1, tn), lambda j, i, f: (layer, 0, j)),
            pl.BlockSpec((gw, gw), lambda j, i, f: (0, 0)),
        ],
        out_specs=pl.BlockSpec((tm, tn), lambda j, i, f: (i, j)),
        scratch_shapes=[pltpu.VMEM((d, tn), BF16)],
    )
    return pl.pallas_call(
        _inproj_kernel,
        grid_spec=grid_spec,
        out_shape=jax.ShapeDtypeStruct((n, width), F32),
        compiler_params=_cparams(("arbitrary", "arbitrary")),
        name="in_proj",
    )(flags, h, w_in, b_in, gain, grp)


def _softmax_pv(scores, values, sink=None):
    m = None
    for s in scores:
        mj = jnp.max(s, axis=-1, keepdims=True)
        m = mj if m is None else jnp.maximum(m, mj)
    if sink is not None:
        m = jnp.maximum(m, sink)
    den = None
    out = None
    for s, v in zip(scores, values):
        p = jnp.exp(s - m)
        dj = jnp.sum(p, axis=-1, keepdims=True)
        oj = jnp.dot(p.astype(BF16), v, preferred_element_type=F32)
        den = dj if den is None else den + dj
        out = oj if out is None else out + oj
    if sink is not None:
        den = den + jnp.exp(sink - m)
    return out / den


def _stack_pair_a(q):
    lo = lax.broadcasted_iota(jnp.int32, q.shape, 1) < HEAD_DIM
    return jnp.concatenate([jnp.where(lo, q, 0.0), jnp.where(lo, 0.0, q)], axis=0).astype(BF16)


def _unstack_pair_a(o, rows):
    lo = lax.broadcasted_iota(jnp.int32, (rows, LANES), 1) < HEAD_DIM
    return jnp.where(lo, o[:rows], o[rows:])


def _stack_group_b(q, n_heads):
    rows = q.shape[0]
    lo = lax.broadcasted_iota(jnp.int32, (rows, LANES), 1) < HEAD_DIM
    per_kv = n_heads // 2
    parts = []
    for i in range(n_heads):
        blk = q[:, (i // 2) * LANES:(i // 2 + 1) * LANES]
        src_half, kv_half = i % 2, i // per_kv
        if src_half != kv_half:
            blk = pltpu.roll(blk, HEAD_DIM, 1)
        parts.append(jnp.where(lo if kv_half == 0 else jnp.logical_not(lo), blk, 0.0))
    return jnp.concatenate(parts, axis=0).astype(BF16)


def _unstack_group_b(o, rows, n_heads):
    lo = lax.broadcasted_iota(jnp.int32, (rows, LANES), 1) < HEAD_DIM
    per_kv = n_heads // 2
    cols = []
    for mblk in range(n_heads // 2):
        halves = []
        for u in range(2):
            i = 2 * mblk + u
            part = o[i * rows:(i + 1) * rows]
            if i // per_kv != u:
                part = pltpu.roll(part, HEAD_DIM, 1)
            halves.append(part)
        cols.append(jnp.where(lo, halves[0], halves[1]))
    return jnp.concatenate(cols, axis=1)


def _attn_a_prompt_kernel(q_ref, k_ref, v_ref, bias_ref, o_ref, *, rq, n_prev):
    qb = pl.program_id(2)
    qs = _stack_pair_a(q_ref[...] * SCALE)
    scores, values = [], []
    for j in range(n_prev + 1):
        blk = qb - n_prev + j
        start = pl.multiple_of(jnp.maximum(blk, 0) * rq, rq)
        kj = k_ref[pl.ds(start, rq), :].astype(BF16)
        values.append(v_ref[pl.ds(start, rq), :].astype(BF16))
        s = _nt_dot(qs, kj) + bias_ref[j]
        if j < n_prev:
            s = jnp.where(blk >= 0, s, NEG)
        scores.append(s)
    o = _softmax_pv(scores, values)
    o_ref[...] = _unstack_pair_a(o, rq).astype(BF16)


def _attn_a_prompt(proj, bias, batch, seq, width_a, rq):
    n_pairs = width_a // LANES
    n_prev = BAND_CHUNKS_A * CHUNK // rq
    nqb = seq // rq
    kcol, vcol = width_a // LANES, 2 * width_a // LANES
    return pl.pallas_call(
        functools.partial(_attn_a_prompt_kernel, rq=rq, n_prev=n_prev),
        grid=(batch, n_pairs, nqb),
        in_specs=[
            pl.BlockSpec((rq, LANES), lambda b, p, i: (b * nqb + i, p)),
            pl.BlockSpec((seq, LANES), lambda b, p, i: (b, kcol + p)),
            pl.BlockSpec((seq, LANES), lambda b, p, i: (b, vcol + p)),
            pl.BlockSpec((None, n_prev + 1, 2 * rq, rq), lambda b, p, i: (p, 0, 0, 0)),
        ],
        out_specs=pl.BlockSpec((rq, LANES), lambda b, p, i: (b * nqb + i, p)),
        out_shape=jax.ShapeDtypeStruct((batch * seq, width_a), BF16),
        compiler_params=_cparams(("arbitrary", "arbitrary", "arbitrary")),
        name="attn_a_prompt",
    )(proj, proj, proj, bias)


def _attn_a_sample_kernel(q_ref, kn_ref, vn_ref, kc_ref, vc_ref, bias_ref, o_ref, *, n_cache):
    qs = _stack_pair_a(q_ref[...] * SCALE)
    bias = bias_ref[...]
    s_c = _nt_dot(qs, kc_ref[...].astype(BF16)) + bias[:, :n_cache]
    s_n = _nt_dot(qs, kn_ref[...].astype(BF16)) + bias[:, n_cache:]
    o = _softmax_pv([s_c, s_n], [vc_ref[...].astype(BF16), vn_ref[...].astype(BF16)])
    o_ref[...] = _unstack_pair_a(o, q_ref.shape[0]).astype(BF16)


def _attn_a_sample(proj, cache_k, cache_v, bias, layer, row0, dec_batch, dec_seq, width_a):
    n_pairs = width_a // LANES
    n_cache = cache_k.shape[2]
    rb0 = row0 // dec_seq
    kcol, vcol = width_a // LANES, 2 * width_a // LANES
    ck = cache_k.reshape(cache_k.shape[0], dec_batch, n_cache, width_a)
    cv = cache_v.reshape(cache_v.shape[0], dec_batch, n_cache, width_a)
    return pl.pallas_call(
        functools.partial(_attn_a_sample_kernel, n_cache=n_cache),
        grid=(dec_batch, n_pairs),
        in_specs=[
            pl.BlockSpec((dec_seq, LANES), lambda b, p: (rb0 + b, p)),
            pl.BlockSpec((dec_seq, LANES), lambda b, p: (rb0 + b, kcol + p)),
            pl.BlockSpec((dec_seq, LANES), lambda b, p: (rb0 + b, vcol + p)),
            pl.BlockSpec((None, None, n_cache, LANES), lambda b, p: (layer, b, 0, p)),
            pl.BlockSpec((None, None, n_cache, LANES), lambda b, p: (layer, b, 0, p)),
            pl.BlockSpec((None, 2 * dec_seq, n_cache + dec_seq), lambda b, p: (p, 0, 0)),
        ],
        out_specs=pl.BlockSpec((dec_seq, LANES), lambda b, p: (b, p)),
        out_shape=jax.ShapeDtypeStruct((dec_batch * dec_seq, width_a), BF16),
        compiler_params=_cparams(("arbitrary", "arbitrary")),
        name="attn_a_sample",
    )(proj, proj, proj, ck, cv, bias)


def _attn_b_prompt_kernel(q_ref, k_ref, v_ref, bias_ref, sink_ref, o_ref, *, rq, n_heads):
    qb = pl.program_id(2)
    qs = _stack_group_b(q_ref[...] * SCALE, n_heads)
    bias = bias_ref[...]
    prev = qb - 1
    p_start = pl.multiple_of(jnp.maximum(prev, 0) * rq, rq)
    o_start = pl.multiple_of(qb * rq, rq)
    s_p = _nt_dot(qs, k_ref[pl.ds(p_start, rq), :].astype(BF16)) + bias[:, :rq]
    s_p = jnp.where(prev >= 0, s_p, NEG)
    s_o = _nt_dot(qs, k_ref[pl.ds(o_start, rq), :].astype(BF16)) + bias[:, rq:]
    o = _softmax_pv([s_p, s_o],
                    [v_ref[pl.ds(p_start, rq), :].astype(BF16),
                     v_ref[pl.ds(o_start, rq), :].astype(BF16)],
                    sink_ref[...])
    o_ref[...] = _unstack_group_b(o, rq, n_heads).astype(BF16)


def _attn_b_prompt(proj, bias, sink, batch, seq, qcol0, kcol0, vcol0, width_b, n_kv_pairs, rq):
    n_heads = width_b // HEAD_DIM // n_kv_pairs
    qw = n_heads * HEAD_DIM
    nqb = seq // rq
    qc, kc, vc = qcol0 // qw, kcol0 // LANES, vcol0 // LANES
    return pl.pallas_call(
        functools.partial(_attn_b_prompt_kernel, rq=rq, n_heads=n_heads),
        grid=(batch, n_kv_pairs, nqb),
        in_specs=[
            pl.BlockSpec((rq, qw), lambda b, p, i: (b * nqb + i, qc + p)),
            pl.BlockSpec((seq, LANES), lambda b, p, i: (b, kc + p)),
            pl.BlockSpec((seq, LANES), lambda b, p, i: (b, vc + p)),
            pl.BlockSpec((None, n_heads * rq, 2 * rq), lambda b, p, i: (p, 0, 0)),
            pl.BlockSpec((None, n_heads * rq, 1), lambda b, p, i: (p, 0, 0)),
        ],
        out_specs=pl.BlockSpec((rq, qw), lambda b, p, i: (b * nqb + i, p)),
        out_shape=jax.ShapeDtypeStruct((batch * seq, width_b), BF16),
        compiler_params=_cparams(("arbitrary", "arbitrary", "arbitrary")),
        name="attn_b_prompt",
    )(proj, proj, proj, bias, sink)


def _attn_b_sample_kernel(q_ref, kn_ref, vn_ref, kc_ref, vc_ref, bias_ref, sink_ref, o_ref, *,
                          n_cache, n_heads):
    qs = _stack_group_b(q_ref[...] * SCALE, n_heads)
    bias = bias_ref[...]
    s_c = _nt_dot(qs, kc_ref[...].astype(BF16)) + bias[:, :n_cache]
    s_n = _nt_dot(qs, kn_ref[...].astype(BF16)) + bias[:, n_cache:]
    o = _softmax_pv([s_c, s_n], [vc_ref[...].astype(BF16), vn_ref[...].astype(BF16)],
                    sink_ref[...])
    o_ref[...] = _unstack_group_b(o, q_ref.shape[0], n_heads).astype(BF16)


def _attn_b_sample(proj, cache_k, cache_v, bias, sink, layer, row0, dec_batch, dec_seq,
                   qcol0, kcol0, vcol0, width_b, n_kv_pairs):
    n_heads = width_b // HEAD_DIM // n_kv_pairs
    qw = n_heads * HEAD_DIM
    n_cache = cache_k.shape[2]
    kvw = cache_k.shape[3] * cache_k.shape[4]
    rb0 = row0 // dec_seq
    qc, kc, vc = qcol0 // qw, kcol0 // LANES, vcol0 // LANES
    ck = cache_k.reshape(cache_k.shape[0], dec_batch, n_cache, kvw)
    cv = cache_v.reshape(cache_v.shape[0], dec_batch, n_cache, kvw)
    return pl.pallas_call(
        functools.partial(_attn_b_sample_kernel, n_cache=n_cache, n_heads=n_heads),
        grid=(dec_batch, n_kv_pairs),
        in_specs=[
            pl.BlockSpec((dec_seq, qw), lambda b, p: (rb0 + b, qc + p)),
            pl.BlockSpec((dec_seq, LANES), lambda b, p: (rb0 + b, kc + p)),
            pl.BlockSpec((dec_seq, LANES), lambda b, p: (rb0 + b, vc + p)),
            pl.BlockSpec((None, None, n_cache, LANES), lambda b, p: (layer, b, 0, p)),
            pl.BlockSpec((None, None, n_cache, LANES), lambda b, p: (layer, b, 0, p)),
            pl.BlockSpec((None, n_heads * dec_seq, n_cache + dec_seq), lambda b, p: (p, 0, 0)),
            pl.BlockSpec((None, n_heads * dec_seq, 1), lambda b, p: (p, 0, 0)),
        ],
        out_specs=pl.BlockSpec((dec_seq, qw), lambda b, p: (b, p)),
        out_shape=jax.ShapeDtypeStruct((dec_batch * dec_seq, width_b), BF16),
        compiler_params=_cparams(("arbitrary", "arbitrary")),
        name="attn_b_sample",
    )(proj, proj, proj, ck, cv, bias, sink)


def _outproj_kernel(a_ref, w_ref, b_ref, x_ref, gate_ref, o_ref, wbf_ref):
    @pl.when(pl.program_id(1) == 0)
    def _():
        wbf_ref[...] = w_ref[...].astype(BF16)

    y = jnp.dot(a_ref[...], wbf_ref[...], preferred_element_type=F32) + b_ref[...]
    tm, tn = y.shape
    y = y.reshape(tm // CHUNK, CHUNK, tn) * gate_ref[...]
    o_ref[...] = x_ref[...] + y.reshape(tm, tn)


def _outproj(attn, w_out, b_out, x, modc, gate_col, layer):
    n, kdim = attn.shape
    d = w_out.shape[-1]
    tm = _pick(n, (512, 256, 128))
    tn = _pick(d, (512, 256, 128))
    cpt = tm // CHUNK
    gc = gate_col * (d // tn)
    return pl.pallas_call(
        _outproj_kernel,
        grid=(d // tn, n // tm),
        in_specs=[
            pl.BlockSpec((tm, kdim), lambda j, i: (i, 0)),
            pl.BlockSpec((None, kdim, tn), lambda j, i: (layer, 0, j)),
            pl.BlockSpec((None, 1, tn), lambda j, i: (layer, 0, j)),
            pl.BlockSpec((tm, tn), lambda j, i: (i, j)),
            pl.BlockSpec((cpt, 1, tn), lambda j, i: (i, 0, gc + j)),
        ],
        out_specs=pl.BlockSpec((tm, tn), lambda j, i: (i, j)),
        out_shape=jax.ShapeDtypeStruct((n, d), F32),
        scratch_shapes=[pltpu.VMEM((kdim, tn), BF16)],
        compiler_params=_cparams(("arbitrary", "arbitrary")),
        name="out_proj",
    )(attn, w_out, b_out, x, modc)


def _router_kernel(x_ref, g_ref, scale_ref, shift_ref, wr_ref, br_ref,
                   hp_ref, e_ref, gate_ref, whi_ref, wlo_ref, *, n_experts):
    @pl.when(pl.program_id(0) == 0)
    def _():
        w = wr_ref[...]
        hi = w.astype(BF16)
        whi_ref[...] = hi
        wlo_ref[...] = (w - hi.astype(F32)).astype(BF16)

    h = _rms_mod(x_ref[...], g_ref[...], scale_ref[...], shift_ref[...])
    tm, d = h.shape
    hi = h.astype(BF16)
    hi32 = hi.astype(F32)
    lo = (h - hi32).astype(BF16)
    bits = lax.bitcast_convert_type(hi32, jnp.uint32)
    hp_ref[...] = bits[:, :d // 2] | (bits[:, d // 2:] >> 16)

    logits = (jnp.dot(hi, whi_ref[...], preferred_element_type=F32)
              + jnp.dot(lo, whi_ref[...], preferred_element_type=F32)
              + jnp.dot(hi, wlo_ref[...], preferred_element_type=F32)) + br_ref[...]

    lane_e = lax.broadcasted_iota(jnp.int32, (tm, n_experts), 1).astype(F32)
    lane_o = lax.broadcasted_iota(jnp.int32, (tm, LANES), 1)
    work = logits
    tops, idxs = [], []
    for _ in range(TOP_K):
        mx = jnp.max(work, axis=-1, keepdims=True)
        ix = jnp.min(jnp.where(work == mx, lane_e, float(n_experts)), axis=-1, keepdims=True)
        tops.append(mx)
        idxs.append(ix)
        work = jnp.where(lane_e == ix, -jnp.inf, work)
    exps = [jnp.exp(t - tops[0]) for t in tops]
    den = exps[0] + exps[1] + exps[2] + exps[3]
    e_out = jnp.zeros((tm, LANES), F32)
    g_out = jnp.zeros((tm, LANES), F32)
    for k in range(TOP_K):
        e_out = jnp.where(lane_o == k, idxs[k], e_out)
        g_out = jnp.where(lane_o == k, exps[k] / den, g_out)
    e_ref[...] = e_out.astype(jnp.int32)
    gate_ref[...] = g_out


def _router(x, g_all, layer, modc, scale_col, shift_col, w_router, b_router):
    n, d = x.shape
    n_experts = w_router.shape[-1]
    tm = _pick(n, (256, 128, 64))
    cpt = tm // CHUNK
    depth = w_router.shape[0]
    return pl.pallas_call(
        functools.partial(_router_kernel, n_experts=n_experts),
        grid=(n // tm,),
        in_specs=[
            pl.BlockSpec((tm, d), lambda i: (i, 0)),
            pl.BlockSpec((None, 1, d), lambda i: (layer, 0, 0)),
            pl.BlockSpec((cpt, 1, d), lambda i: (i, 0, scale_col)),
            pl.BlockSpec((cpt, 1, d), lambda i: (i, 0, shift_col)),
            pl.BlockSpec((None, d, n_experts), lambda i: (layer, 0, 0)),
            pl.BlockSpec((None, 1, n_experts), lambda i: (layer, 0, 0)),
        ],
        out_specs=[
            pl.BlockSpec((tm, d // 2), lambda i: (i, 0)),
            pl.BlockSpec((tm, LANES), lambda i: (i, 0)),
            pl.BlockSpec((tm, LANES), lambda i: (i, 0)),
        ],
        out_shape=[
            jax.ShapeDtypeStruct((n, d // 2), jnp.uint32),
            jax.ShapeDtypeStruct((n, LANES), jnp.int32),
            jax.ShapeDtypeStruct((n, LANES), F32),
        ],
        scratch_shapes=[pltpu.VMEM((d, n_experts), BF16), pltpu.VMEM((d, n_experts), BF16)],
        compiler_params=_cparams(("arbitrary",)),
        name="norm_router",
    )(x, g_all, modc, modc, w_router, b_router.reshape(depth, 1, n_experts))


DMA_ISSUE_UNROLL = 8


def _row_copy(src, dst, sem, src_row, dst_row):
    return pltpu.make_async_copy(src.at[pl.ds(src_row, 1)], dst.at[pl.ds(dst_row, 1)], sem)


def _tile_wait(src, dst, sem, tm):
    pltpu.make_async_copy(src.at[pl.ds(0, tm)], dst.at[pl.ds(0, tm)], sem).wait()


def _gather_kernel(tok_ref, h_hbm, o_hbm, buf_ref, in_sem, out_sem, *, tm):
    i = pl.program_id(0)
    n_steps = pl.num_programs(0)
    slot = i % 2

    def issue_tile(tile, s):
        base = tile * tm

        def issue(r, carry):
            _row_copy(h_hbm, buf_ref.at[s], in_sem.at[s], tok_ref[base + r], r).start()
            return carry

        lax.fori_loop(0, tm, issue, 0, unroll=DMA_ISSUE_UNROLL)

    def out_copy(tile, s):
        rows = pl.ds(pl.multiple_of(tile * tm, tm), tm)
        return pltpu.make_async_copy(buf_ref.at[s], o_hbm.at[rows], out_sem.at[s])

    @pl.when(i == 0)
    def _():
        issue_tile(0, 0)

    @pl.when(i >= 1)
    def _():
        out_copy(i - 1, 1 - slot).wait()

    @pl.when(i + 1 < n_steps)
    def _():
        issue_tile(i + 1, 1 - slot)

    _tile_wait(h_hbm, buf_ref.at[slot], in_sem.at[slot], tm)
    out_copy(i, slot).start()

    @pl.when(i == n_steps - 1)
    def _():
        out_copy(i, slot).wait()


def _gather_rows(row_tok, hp, tm):
    n_rows = row_tok.shape[0]
    grid_spec = pltpu.PrefetchScalarGridSpec(
        num_scalar_prefetch=1,
        grid=(n_rows // tm,),
        in_specs=[pl.BlockSpec(memory_space=pl.ANY)],
        out_specs=pl.BlockSpec(memory_space=pl.ANY),
        scratch_shapes=[pltpu.VMEM((2, tm, hp.shape[1]), hp.dtype),
                        pltpu.SemaphoreType.DMA((2,)), pltpu.SemaphoreType.DMA((2,))],
    )
    return pl.pallas_call(
        functools.partial(_gather_kernel, tm=tm),
        grid_spec=grid_spec,
        out_shape=jax.ShapeDtypeStruct((n_rows, hp.shape[1]), hp.dtype),
        compiler_params=_cparams(("arbitrary",)),
        name="moe_gather",
    )(row_tok, hp)


def _unpack_rows(words):
    first = lax.bitcast_convert_type(words & jnp.uint32(0xFFFF0000), F32).astype(BF16)
    second = lax.bitcast_convert_type(words << 16, F32).astype(BF16)
    return first, second


def _expert_tile_loop(ts_ref, tc_ref, nu_ref, src_hbm, dst_hbm, ibuf, obuf, in_sem, out_sem,
                      compute, *, tm, tn):
    j, e = pl.program_id(0), pl.program_id(1)
    first, count = ts_ref[e], tc_ref[e]
    cols = pl.ds(pl.multiple_of(j * tn, tn), tn)

    def rows(tile):
        return pl.ds(pl.multiple_of(tile * tm, tm), tm)

    def load(t, s):
        return pltpu.make_async_copy(src_hbm.at[rows(first + t)], ibuf.at[s], in_sem.at[s])

    def store(tile, s):
        return pltpu.make_async_copy(obuf.at[s], dst_hbm.at[rows(tile), cols], out_sem.at[s])

    @pl.when(count > 0)
    def _():
        load(0, 0).start()

    def body(t, carry):
        s = t % 2
        load(t, s).wait()

        @pl.when(t + 1 < count)
        def _():
            load(t + 1, 1 - s).start()

        @pl.when(t >= 2)
        def _():
            store(first + t - 2, s).wait()

        obuf[s] = compute(ibuf[s])
        store(first + t, s).start()
        return carry

    lax.fori_loop(0, count, body, 0)

    @pl.when(count >= 2)
    def _():
        store(first + count - 2, count % 2).wait()

    @pl.when(count >= 1)
    def _():
        store(first + count - 1, (count - 1) % 2).wait()

    @pl.when(e == pl.num_programs(1) - 1)
    def _():
        obuf[0] = jnp.zeros(obuf.shape[1:], obuf.dtype)

        def fill(tile, carry):
            cp = store(tile, 0)
            cp.start()
            cp.wait()
            return carry

        lax.fori_loop(nu_ref[0], dst_hbm.shape[0] // tm, fill, 0)


def _up_kernel(ts_ref, tc_ref, nu_ref, x_hbm, wg_ref, wl_ref, bg_ref, bl_ref, o_hbm,
               xbuf, obuf, wgb_ref, wlb_ref, in_sem, out_sem, *, tm, tn):
    wgb_ref[...] = wg_ref[...].astype(BF16)
    wlb_ref[...] = wl_ref[...].astype(BF16)

    def compute(words):
        x0, x1 = _unpack_rows(words)
        half = x0.shape[1]
        glu = (jnp.dot(x0, wgb_ref[:half, :], preferred_element_type=F32)
               + jnp.dot(x1, wgb_ref[half:, :], preferred_element_type=F32)) + bg_ref[...]
        lin = (jnp.dot(x0, wlb_ref[:half, :], preferred_element_type=F32)
               + jnp.dot(x1, wlb_ref[half:, :], preferred_element_type=F32)) + bl_ref[...]
        glu = jnp.minimum(glu, SWIGLU_LIMIT)
        lin = jnp.clip(lin, -SWIGLU_LIMIT, SWIGLU_LIMIT)
        act = glu * (1.0 / (1.0 + jnp.exp(-SWIGLU_ALPHA * glu))) * (lin + 1.0)
        return act.astype(BF16)

    _expert_tile_loop(ts_ref, tc_ref, nu_ref, x_hbm, o_hbm, xbuf, obuf, in_sem, out_sem, compute,
                      tm=tm, tn=tn)


def _expert_up(tile_first, tile_count, n_used, xs, w_up, b_up, layer, tm):
    depth, n_experts, d, two_ff = w_up.shape
    n_rows, half = xs.shape
    d_ff = two_ff // 2
    tn = _pick(d_ff, (512, 256, 128))
    nj = d_ff // tn
    b4 = b_up.reshape(depth, n_experts, 1, two_ff)
    grid_spec = pltpu.PrefetchScalarGridSpec(
        num_scalar_prefetch=3,
        grid=(nj, n_experts),
        in_specs=[
            pl.BlockSpec(memory_space=pl.ANY),
            pl.BlockSpec((None, None, d, tn), lambda j, e, *_: (layer, e, 0, j)),
            pl.BlockSpec((None, None, d, tn), lambda j, e, *_: (layer, e, 0, nj + j)),
            pl.BlockSpec((None, None, 1, tn), lambda j, e, *_: (layer, e, 0, j)),
            pl.BlockSpec((None, None, 1, tn), lambda j, e, *_: (layer, e, 0, nj + j)),
        ],
        out_specs=pl.BlockSpec(memory_space=pl.ANY),
        scratch_shapes=[pltpu.VMEM((2, tm, half), xs.dtype), pltpu.VMEM((2, tm, tn), BF16),
                        pltpu.VMEM((d, tn), BF16), pltpu.VMEM((d, tn), BF16),
                        pltpu.SemaphoreType.DMA((2,)), pltpu.SemaphoreType.DMA((2,))],
    )
    return pl.pallas_call(
        functools.partial(_up_kernel, tm=tm, tn=tn),
        grid_spec=grid_spec,
        out_shape=jax.ShapeDtypeStruct((n_rows, d_ff), BF16),
        compiler_params=_cparams(("arbitrary", "arbitrary")),
        name="expert_up",
    )(tile_first, tile_count, n_used, xs, w_up, w_up, b4, b4)


def _down_kernel(ts_ref, tc_ref, nu_ref, a_hbm, w_ref, b_ref, o_hbm,
                 abuf, obuf, wb_ref, in_sem, out_sem, *, tm, tn):
    wb_ref[...] = w_ref[...].astype(BF16)

    def compute(a):
        return jnp.dot(a, wb_ref[...], preferred_element_type=F32) + b_ref[...]

    _expert_tile_loop(ts_ref, tc_ref, nu_ref, a_hbm, o_hbm, abuf, obuf, in_sem, out_sem, compute,
                      tm=tm, tn=tn)


def _expert_down(tile_first, tile_count, n_used, act, w_down, b_down, layer, tm):
    n_rows, d_ff = act.shape
    depth, n_experts, _, d = w_down.shape
    tn = _pick(d, (2048, 1024, 512, 256, 128))
    b4 = b_down.reshape(depth, n_experts, 1, d)
    grid_spec = pltpu.PrefetchScalarGridSpec(
        num_scalar_prefetch=3,
        grid=(d // tn, n_experts),
        in_specs=[
            pl.BlockSpec(memory_space=pl.ANY),
            pl.BlockSpec((None, None, d_ff, tn), lambda j, e, *_: (layer, e, 0, j)),
            pl.BlockSpec((None, None, 1, tn), lambda j, e, *_: (layer, e, 0, j)),
        ],
        out_specs=pl.BlockSpec(memory_space=pl.ANY),
        scratch_shapes=[pltpu.VMEM((2, tm, d_ff), act.dtype), pltpu.VMEM((2, tm, tn), F32),
                        pltpu.VMEM((d_ff, tn), BF16),
                        pltpu.SemaphoreType.DMA((2,)), pltpu.SemaphoreType.DMA((2,))],
    )
    return pl.pallas_call(
        functools.partial(_down_kernel, tm=tm, tn=tn),
        grid_spec=grid_spec,
        out_shape=jax.ShapeDtypeStruct((n_rows, d), F32),
        compiler_params=_cparams(("arbitrary", "arbitrary")),
        name="expert_down",
    )(tile_first, tile_count, n_used, act, w_down, b4)


def _combine_kernel(pos_ref, gate_ref, x_ref, g2_ref, y_hbm, o_ref, buf_ref, sem, *, tm):
    i = pl.program_id(0)
    n_steps = pl.num_programs(0)

    def issue_tile(tile, slot):
        base = tile * (tm * TOP_K)

        def issue(r, carry):
            for k in range(TOP_K):
                _row_copy(y_hbm, buf_ref.at[slot, k], sem.at[slot],
                          pos_ref[base + r * TOP_K + k], r).start()
            return carry

        lax.fori_loop(0, tm, issue, 0, unroll=DMA_ISSUE_UNROLL // TOP_K)

    slot = i % 2

    @pl.when(i == 0)
    def _():
        issue_tile(0, 0)

    @pl.when(i + 1 < n_steps)
    def _():
        issue_tile(i + 1, 1 - slot)

    for k in range(TOP_K):
        _tile_wait(y_hbm, buf_ref.at[slot, k], sem.at[slot], tm)

    gate = gate_ref[...]
    acc = gate[:, 0:1] * buf_ref[slot, 0]
    for k in range(1, TOP_K):
        acc = acc + gate[:, k:k + 1] * buf_ref[slot, k]
    d = acc.shape[1]
    acc = acc.reshape(tm // CHUNK, CHUNK, d) * g2_ref[...]
    o_ref[...] = x_ref[...] + acc.reshape(tm, d)


def _combine(pos, gate, x, modc, gate_col, yb):
    n, d = x.shape
    tm = _pick(n, (128, 64))
    cpt = tm // CHUNK
    grid_spec = pltpu.PrefetchScalarGridSpec(
        num_scalar_prefetch=1,
        grid=(n // tm,),
        in_specs=[
            pl.BlockSpec((tm, LANES), lambda i, p: (i, 0)),
            pl.BlockSpec((tm, d), lambda i, p: (i, 0)),
            pl.BlockSpec((cpt, 1, d), lambda i, p: (i, 0, gate_col)),
            pl.BlockSpec(memory_space=pl.ANY),
        ],
        out_specs=pl.BlockSpec((tm, d), lambda i, p: (i, 0)),
        scratch_shapes=[pltpu.VMEM((2, TOP_K, tm, d), F32), pltpu.SemaphoreType.DMA((2,))],
    )
    return pl.pallas_call(
        functools.partial(_combine_kernel, tm=tm),
        grid_spec=grid_spec,
        out_shape=jax.ShapeDtypeStruct((n, d), F32),
        compiler_params=_cparams(("arbitrary",)),
        name="moe_combine",
    )(pos, gate, x, modc, yb)


def _dispatch_plan(top_e, n_experts, tm):
    n_tok = top_e.shape[0]
    n_assign = n_tok * TOP_K
    flat_e = top_e.reshape(n_assign)
    onehot = (flat_e[:, None] == jnp.arange(n_experts, dtype=jnp.int32)[None, :]).astype(jnp.int32)
    csum = jnp.cumsum(onehot, axis=0)
    counts = csum[-1]
    rank = jnp.take_along_axis(csum, flat_e[:, None], axis=1)[:, 0] - 1
    padded = (counts + tm - 1) // tm * tm
    p_end = jnp.cumsum(padded)
    p_start = p_end - padded
    dest = (p_start[flat_e] + rank).astype(jnp.int32)
    n_tiles = n_assign // tm + n_experts
    row_tok = jnp.zeros((n_tiles * tm,), jnp.int32).at[dest].set(
        jnp.arange(n_assign, dtype=jnp.int32) // TOP_K)
    tile_first = (p_start // tm).astype(jnp.int32)
    tile_count = (padded // tm).astype(jnp.int32)
    n_used = (p_end[-1] // tm).astype(jnp.int32).reshape(1)
    return dest, row_tok, tile_first, tile_count, n_used


def _bias_a(rel_bias, rq, n_keys, masked):
    n_heads = rel_bias.shape[0]
    qi = jnp.arange(rq)[:, None]
    ki = jnp.arange(n_keys)[None, :]
    off = n_keys - rq
    n_diag = n_keys + rq - 1
    rel = jnp.clip(jnp.arange(n_diag) - (rq - 1) - off, -REL_CLIP, REL_CLIP) + REL_CLIP
    ext = jnp.pad(jnp.take(rel_bias, rel, axis=1), ((0, 0), (0, 1)))
    skew = jnp.tile(ext, (1, rq))[:, :rq * n_diag].reshape(n_heads, rq, n_diag)
    bias = skew[:, :, rq - 1:rq - 1 + n_keys]
    if masked:
        qc = (qi + off) // CHUNK
        kc = ki // CHUNK
        ok = (kc <= qc) & (kc >= qc - BAND_CHUNKS_A)
        bias = jnp.where(ok[None], bias, NEG)
    return bias.reshape(n_heads // 2, 2 * rq, n_keys)


def _bias_b(n_heads_total, n_kv_pairs, rq, n_keys, masked):
    slopes = 2.0 ** (-8.0 * jnp.arange(1, n_heads_total + 1, dtype=F32) / n_heads_total)
    qi = jnp.arange(rq)[:, None]
    ki = jnp.arange(n_keys)[None, :]
    off = n_keys - rq
    dist = jnp.abs(qi - (ki - off)).astype(F32)
    bias = -slopes[:, None, None] * dist[None]
    if masked:
        qc = (qi + off) // CHUNK
        kc = ki // CHUNK
        ok = (kc <= qc) & (kc >= qc - BAND_CHUNKS_B)
        bias = jnp.where(ok[None], bias, NEG)
    return bias.reshape(n_kv_pairs, (n_heads_total // n_kv_pairs) * rq, n_keys)


def kernel(x_prompt, x_sample, c_prompt, c_sample, cache_k_a, cache_v_a, cache_k_b, cache_v_b,
           w_ada, b_ada, g_attn, g_ffn, w_in, b_in, g_q_a, g_k_a, g_q_b, g_k_b, rel_bias_a,
           sinks_b, w_out, b_out, w_router, b_router, w_up, b_up, w_down, b_down):
    batch, seq, d = x_prompt.shape
    dec_batch, dec_seq, _ = x_sample.shape
    depth = w_in.shape[0]
    heads_a = rel_bias_a.shape[1]
    heads_b = sinks_b.shape[1]
    kv_heads_b = cache_k_b.shape[3]
    n_experts = w_router.shape[-1]
    width_a, width_b, kv_width_b = heads_a * HEAD_DIM, heads_b * HEAD_DIM, kv_heads_b * HEAD_DIM
    in_width = w_in.shape[-1]
    n_cache_a, n_cache_b = cache_k_a.shape[2], cache_k_b.shape[2]
    band_a, band_b = BAND_CHUNKS_A * CHUNK, BAND_CHUNKS_B * CHUNK
    assert dec_seq == CHUNK and PAST_LEN % CHUNK == 0 and seq % CHUNK == 0
    assert n_cache_a == band_a and n_cache_b == band_b
    assert in_width == 3 * width_a + width_b + 2 * kv_width_b and width_a + width_b == d
    assert kv_heads_b % 2 == 0 and heads_a % 2 == 0

    n_p, n_s = batch * seq, dec_batch * dec_seq
    n_tok = n_p + n_s
    x = jnp.concatenate([x_prompt.reshape(n_p, d), x_sample.reshape(n_s, d)], axis=0)

    n_seq = batch + dec_batch
    c_all = jnp.concatenate([c_prompt, c_sample], axis=0)
    c_pad = jnp.pad(c_all, ((0, -n_seq % 8), (0, 0)))
    mod_all = _ada(c_pad, w_ada, b_ada)
    shift1_c, scale1_c, gate1_c, shift2_c, scale2_c, gate2_c = range(N_MOD)

    seg = (width_a, 2 * width_a, 3 * width_a, 3 * width_a + width_b,
           3 * width_a + width_b + kv_width_b, in_width)
    tn_in = _pick(math.gcd(*seg), (512, 256, 128))
    col = jnp.arange(in_width)
    is_v = ((col >= seg[1]) & (col < seg[2])) | (col >= seg[4])
    flags = jnp.logical_not(is_v[::tn_in]).astype(jnp.int32)

    def gains(l):
        ones_a, ones_kv = jnp.ones((width_a,), F32), jnp.ones((kv_width_b,), F32)
        return jnp.concatenate([jnp.tile(g_q_a[l], heads_a), jnp.tile(g_k_a[l], heads_a), ones_a,
                                jnp.tile(g_q_b[l], heads_b), jnp.tile(g_k_b[l], kv_heads_b),
                                ones_kv])

    gain_all = jnp.stack([gains(l) for l in range(depth)]).reshape(depth, 1, in_width)
    b_in3 = b_in.reshape(depth, 1, in_width)
    b_out3 = b_out.reshape(depth, 1, d)
    g_attn3 = g_attn.reshape(depth, 1, d)
    g_ffn3 = g_ffn.reshape(depth, 1, d)

    rq_a = 4 * CHUNK
    rq_b = 2 * CHUNK
    n_kv_pairs = kv_heads_b // 2
    heads_per_pair = heads_b // n_kv_pairs
    bias_bp = _bias_b(heads_b, n_kv_pairs, rq_b, 2 * rq_b, True)
    bias_bs = _bias_b(heads_b, n_kv_pairs, dec_seq, n_cache_b + dec_seq, False)

    tm_moe = 256
    new_ka_p, new_va_p, new_kb_p, new_vb_p = [], [], [], []
    new_ka_s, new_va_s, new_kb_s, new_vb_s = [], [], [], []
    for l in range(depth):
        mod_p = jnp.broadcast_to(mod_all[l, :batch, None, :], (batch, seq // CHUNK, N_MOD * d))
        modc = jnp.concatenate([mod_p.reshape(n_p // CHUNK, N_MOD * d), mod_all[l, batch:n_seq]],
                               axis=0).reshape(n_tok // CHUNK, 1, N_MOD * d)
        h = _norm(x, g_attn3, l, modc, scale1_c, shift1_c)
        proj = _inproj(h, w_in, b_in3, gain_all, flags, l, tn_in)

        bias_ap = _bias_a(rel_bias_a[l], rq_a, band_a + rq_a, True)
        bias_ap = bias_ap.reshape(heads_a // 2, 2 * rq_a, -1, rq_a).transpose(0, 2, 1, 3)
        bias_as = _bias_a(rel_bias_a[l], dec_seq, n_cache_a + dec_seq, False)
        sink_p = jnp.repeat(sinks_b[l], rq_b).reshape(n_kv_pairs, heads_per_pair * rq_b, 1)
        sink_s = jnp.repeat(sinks_b[l], dec_seq).reshape(n_kv_pairs, heads_per_pair * dec_seq, 1)

        oa_p = _attn_a_prompt(proj, bias_ap, batch, seq, width_a, rq_a)
        oa_s = _attn_a_sample(proj, cache_k_a, cache_v_a, bias_as, l, n_p, dec_batch, dec_seq,
                              width_a)
        ob_p = _attn_b_prompt(proj, bias_bp, sink_p, batch, seq, seg[2], seg[3], seg[4],
                              width_b, n_kv_pairs, rq_b)
        ob_s = _attn_b_sample(proj, cache_k_b, cache_v_b, bias_bs, sink_s, l, n_p, dec_batch,
                              dec_seq, seg[2], seg[3], seg[4], width_b, n_kv_pairs)
        attn = jnp.concatenate([jnp.concatenate([oa_p, ob_p], axis=1),
                                jnp.concatenate([oa_s, ob_s], axis=1)], axis=0)
        x = _outproj(attn, w_out, b_out3, x, modc, gate1_c, l)

        hp, e_pad, gate_pad = _router(x, g_ffn3, l, modc, scale2_c, shift2_c, w_router, b_router)
        dest, row_tok, tile_first, tile_count, n_used = _dispatch_plan(e_pad[:, :TOP_K], n_experts,
                                                                       tm_moe)
        xs = _gather_rows(row_tok, hp, tm_moe)
        act = _expert_up(tile_first, tile_count, n_used, xs, w_up, b_up, l, tm_moe)
        yb = _expert_down(tile_first, tile_count, n_used, act, w_down, b_down, l, tm_moe)
        x = _combine(dest, gate_pad, x, modc, gate2_c, yb)

        ka = proj[:, seg[0]:seg[1]]
        va = proj[:, seg[1]:seg[2]]
        kb = proj[:, seg[3]:seg[4]]
        vb = proj[:, seg[4]:seg[5]]
        for dst, src, nh, keep in ((new_ka_p, ka, heads_a, band_a), (new_va_p, va, heads_a, band_a),
                                   (new_kb_p, kb, kv_heads_b, band_b),
                                   (new_vb_p, vb, kv_heads_b, band_b)):
            last = jnp.stack([src[(b + 1) * seq - keep:(b + 1) * seq] for b in range(batch)])
            dst.append(last.reshape(batch, keep, nh, HEAD_DIM))
        for dst, src, nh in ((new_ka_s, ka, heads_a), (new_va_s, va, heads_a),
                             (new_kb_s, kb, kv_heads_b), (new_vb_s, vb, kv_heads_b)):
            dst.append(src[n_p:].reshape(dec_batch, dec_seq, nh, HEAD_DIM))

    return (x[:n_p].reshape(batch, seq, d), x[n_p:].reshape(dec_batch, dec_seq, d),
            jnp.stack(new_ka_p), jnp.stack(new_va_p), jnp.stack(new_kb_p), jnp.stack(new_vb_p),
            jnp.stack(new_ka_s), jnp.stack(new_va_s), jnp.stack(new_kb_s), jnp.stack(new_vb_s))
```

```python
import functools
import math

import jax
import jax.numpy as jnp
from jax import lax
from jax.experimental import pallas as pl
from jax.experimental.pallas import tpu as pltpu

CHUNK = 64
HEAD_DIM = 64
LANES = 128
MXU_WIDTH = 256
TOP_K = 4
BAND_CHUNKS_A = 8
BAND_CHUNKS_B = 2
REL_CLIP = 128
PAST_LEN = 1024
RMS_EPS = 1e-5
SWIGLU_ALPHA = 1.702
SWIGLU_LIMIT = 7.0
SCALE = HEAD_DIM ** -0.5
NEG = -1e30
N_MOD = 6
VMEM_LIMIT = 56 * 1024 * 1024

F32 = jnp.float32
BF16 = jnp.bfloat16


def _cparams(sem):
    return pltpu.CompilerParams(dimension_semantics=sem, vmem_limit_bytes=VMEM_LIMIT)


def _pick(n, cands):
    for c in cands:
        if n % c == 0:
            return c
    raise ValueError(f"no tile in {cands} divides {n}")


def _nt_dot(a, b):
    return lax.dot_general(a, b, (((1,), (1,)), ((), ())), preferred_element_type=F32)


def _ada_kernel(c_ref, w_ref, b_ref, o_ref):
    c = c_ref[...]
    s = (c * (1.0 / (1.0 + jnp.exp(-c)))).astype(BF16)
    o_ref[...] = jnp.dot(s, w_ref[...].astype(BF16), preferred_element_type=F32) + b_ref[...]


def _ada(c_pad, w_ada, b_ada):
    depth, d, n = w_ada.shape
    rows = c_pad.shape[0]
    tn = _pick(n, (512, 256, 128))
    return pl.pallas_call(
        _ada_kernel,
        grid=(depth, n // tn),
        in_specs=[
            pl.BlockSpec((rows, d), lambda l, j: (0, 0)),
            pl.BlockSpec((None, d, tn), lambda l, j: (l, 0, j)),
            pl.BlockSpec((None, 1, tn), lambda l, j: (l, 0, j)),
        ],
        out_specs=pl.BlockSpec((None, rows, tn), lambda l, j: (l, 0, j)),
        out_shape=jax.ShapeDtypeStruct((depth, rows, n), F32),
        compiler_params=_cparams(("arbitrary", "arbitrary")),
        name="ada_mod",
    )(c_pad, w_ada, b_ada.reshape(depth, 1, n))


def _rms_mod(x, g, scale, shift):
    tm, d = x.shape
    ms = jnp.mean(x * x, axis=-1, keepdims=True)
    y = x * lax.rsqrt(ms + RMS_EPS) * g
    y = y.reshape(tm // CHUNK, CHUNK, d)
    return (y * (1.0 + scale) + shift).reshape(tm, d)


def _norm_kernel(x_ref, g_ref, scale_ref, shift_ref, h_ref):
    h_ref[...] = _rms_mod(x_ref[...], g_ref[...], scale_ref[...], shift_ref[...]).astype(BF16)


def _norm(x, g_all, layer, modc, scale_col, shift_col):
    n, d = x.shape
    tm = _pick(n, (256, 128, 64))
    cpt = tm // CHUNK
    return pl.pallas_call(
        _norm_kernel,
        grid=(n // tm,),
        in_specs=[
            pl.BlockSpec((tm, d), lambda i: (i, 0)),
            pl.BlockSpec((None, 1, d), lambda i: (layer, 0, 0)),
            pl.BlockSpec((cpt, 1, d), lambda i: (i, 0, scale_col)),
            pl.BlockSpec((cpt, 1, d), lambda i: (i, 0, shift_col)),
        ],
        out_specs=pl.BlockSpec((tm, d), lambda i: (i, 0)),
        out_shape=jax.ShapeDtypeStruct((n, d), BF16),
        compiler_params=_cparams(("arbitrary",)),
        name="norm_mod",
    )(x, g_all, modc, modc)


def _inproj_kernel(flag_ref, h_ref, w_ref, b_ref, gain_ref, grp_ref, o_ref, wbf_ref):
    j = pl.program_id(0)

    @pl.when(pl.program_id(1) == 0)
    def _():
        wbf_ref[...] = w_ref[...].astype(BF16)

    y = jnp.dot(h_ref[...], wbf_ref[...], preferred_element_type=F32) + b_ref[...]

    @pl.when(flag_ref[j] == 0)
    def _():
        o_ref[...] = y

    @pl.when(flag_ref[j] != 0)
    def _():
        grp = grp_ref[...]
        gw = grp.shape[0]
        for c in range(y.shape[1] // gw):
            yc = y[:, c * gw:(c + 1) * gw]
            ss = jnp.dot((yc * yc).astype(BF16), grp, preferred_element_type=F32)
            o_ref[:, c * gw:(c + 1) * gw] = (yc * lax.rsqrt(ss * (1.0 / HEAD_DIM) + RMS_EPS)
                                             * gain_ref[:, c * gw:(c + 1) * gw])


def _inproj(h, w_in, b_in, gain, flags, layer, tn):
    n, d = h.shape
    width = w_in.shape[-1]
    tm = _pick(n, (512, 256, 128))
    gw = min(tn, MXU_WIDTH)
    grp = (jnp.arange(gw)[:, None] // HEAD_DIM == jnp.arange(gw)[None, :] // HEAD_DIM).astype(BF16)
    grid_spec = pltpu.PrefetchScalarGridSpec(
        num_scalar_prefetch=1,
        grid=(width // tn, n // tm),
        in_specs=[
            pl.BlockSpec((tm, d), lambda j, i, f: (i, 0)),
            pl.BlockSpec((None, d, tn), lambda j, i, f: (layer, 0, j)),
            pl.BlockSpec((None, 1, tn), lambda j, i, f: (layer, 0, j)),
            pl.BlockSpec((None, 1, tn), lambda j, i, f: (layer, 0, j)),
            pl.BlockSpec((gw, gw), lambda j, i, f: (0, 0)),
        ],
        out_specs=pl.BlockSpec((tm, tn), lambda j, i, f: (i, j)),
        scratch_shapes=[pltpu.VMEM((d, tn), BF16)],
    )
    return pl.pallas_call(
        _inproj_kernel,
        grid_spec=grid_spec,
        out_shape=jax.ShapeDtypeStruct((n, width), F32),
        compiler_params=_cparams(("arbitrary", "arbitrary")),
        name="in_proj",
    )(flags, h, w_in, b_in, gain, grp)


def _softmax_pv(scores, values, sink=None):
    m = None
    for s in scores:
        mj = jnp.max(s, axis=-1, keepdims=True)
        m = mj if m is None else jnp.maximum(m, mj)
    if sink is not None:
        m = jnp.maximum(m, sink)
    den = None
    out = None
    for s, v in zip(scores, values):
        p = jnp.exp(s - m)
        dj = jnp.sum(p, axis=-1, keepdims=True)
        oj = jnp.dot(p.astype(BF16), v, preferred_element_type=F32)
        den = dj if den is None else den + dj
        out = oj if out is None else out + oj
    if sink is not None:
        den = den + jnp.exp(sink - m)
    return out / den


def _stack_pair_a(q):
    lo = lax.broadcasted_iota(jnp.int32, q.shape, 1) < HEAD_DIM
    return jnp.concatenate([jnp.where(lo, q, 0.0), jnp.where(lo, 0.0, q)], axis=0).astype(BF16)


def _unstack_pair_a(o, rows):
    lo = lax.broadcasted_iota(jnp.int32, (rows, LANES), 1) < HEAD_DIM
    return jnp.where(lo, o[:rows], o[rows:])


def _stack_group_b(q, n_heads):
    rows = q.shape[0]
    lo = lax.broadcasted_iota(jnp.int32, (rows, LANES), 1) < HEAD_DIM
    per_kv = n_heads // 2
    parts = []
    for i in range(n_heads):
        blk = q[:, (i // 2) * LANES:(i // 2 + 1) * LANES]
        src_half, kv_half = i % 2, i // per_kv
        if src_half != kv_half:
            blk = pltpu.roll(blk, HEAD_DIM, 1)
        parts.append(jnp.where(lo if kv_half == 0 else jnp.logical_not(lo), blk, 0.0))
    return jnp.concatenate(parts, axis=0).astype(BF16)


def _unstack_group_b(o, rows, n_heads):
    lo = lax.broadcasted_iota(jnp.int32, (rows, LANES), 1) < HEAD_DIM
    per_kv = n_heads // 2
    cols = []
    for mblk in range(n_heads // 2):
        halves = []
        for u in range(2):
            i = 2 * mblk + u
            part = o[i * rows:(i + 1) * rows]
            if i // per_kv != u:
                part = pltpu.roll(part, HEAD_DIM, 1)
            halves.append(part)
        cols.append(jnp.where(lo, halves[0], halves[1]))
    return jnp.concatenate(cols, axis=1)


def _attn_a_prompt_kernel(q_ref, k_ref, v_ref, bias_ref, o_ref, *, rq, n_prev):
    qb = pl.program_id(2)
    qs = _stack_pair_a(q_ref[...] * SCALE)
    scores, values = [], []
    for j in range(n_prev + 1):
        blk = qb - n_prev + j
        start = pl.multiple_of(jnp.maximum(blk, 0) * rq, rq)
        kj = k_ref[pl.ds(start, rq), :].astype(BF16)
        values.append(v_ref[pl.ds(start, rq), :].astype(BF16))
        s = _nt_dot(qs, kj) + bias_ref[j]
        if j < n_prev:
            s = jnp.where(blk >= 0, s, NEG)
        scores.append(s)
    o = _softmax_pv(scores, values)
    o_ref[...] = _unstack_pair_a(o, rq).astype(BF16)


def _attn_a_prompt(proj, bias, batch, seq, width_a, rq):
    n_pairs = width_a // LANES
    n_prev = BAND_CHUNKS_A * CHUNK // rq
    nqb = seq // rq
    kcol, vcol = width_a // LANES, 2 * width_a // LANES
    return pl.pallas_call(
        functools.partial(_attn_a_prompt_kernel, rq=rq, n_prev=n_prev),
        grid=(batch, n_pairs, nqb),
        in_specs=[
            pl.BlockSpec((rq, LANES), lambda b, p, i: (b * nqb + i, p)),
            pl.BlockSpec((seq, LANES), lambda b, p, i: (b, kcol + p)),
            pl.BlockSpec((seq, LANES), lambda b, p, i: (b, vcol + p)),
            pl.BlockSpec((None, n_prev + 1, 2 * rq, rq), lambda b, p, i: (p, 0, 0, 0)),
        ],
        out_specs=pl.BlockSpec((rq, LANES), lambda b, p, i: (b * nqb + i, p)),
        out_shape=jax.ShapeDtypeStruct((batch * seq, width_a), BF16),
        compiler_params=_cparams(("arbitrary", "arbitrary", "arbitrary")),
        name="attn_a_prompt",
    )(proj, proj, proj, bias)


def _attn_a_sample_kernel(q_ref, kn_ref, vn_ref, kc_ref, vc_ref, bias_ref, o_ref, *, n_cache):
    qs = _stack_pair_a(q_ref[...] * SCALE)
    bias = bias_ref[...]
    s_c = _nt_dot(qs, kc_ref[...].astype(BF16)) + bias[:, :n_cache]
    s_n = _nt_dot(qs, kn_ref[...].astype(BF16)) + bias[:, n_cache:]
    o = _softmax_pv([s_c, s_n], [vc_ref[...].astype(BF16), vn_ref[...].astype(BF16)])
    o_ref[...] = _unstack_pair_a(o, q_ref.shape[0]).astype(BF16)


def _attn_a_sample(proj, cache_k, cache_v, bias, layer, row0, dec_batch, dec_seq, width_a):
    n_pairs = width_a // LANES
    n_cache = cache_k.shape[2]
    rb0 = row0 // dec_seq
    kcol, vcol = width_a // LANES, 2 * width_a // LANES
    ck = cache_k.reshape(cache_k.shape[0], dec_batch, n_cache, width_a)
    cv = cache_v.reshape(cache_v.shape[0], dec_batch, n_cache, width_a)
    return pl.pallas_call(
        functools.partial(_attn_a_sample_kernel, n_cache=n_cache),
        grid=(dec_batch, n_pairs),
        in_specs=[
            pl.BlockSpec((dec_seq, LANES), lambda b, p: (rb0 + b, p)),
            pl.BlockSpec((dec_seq, LANES), lambda b, p: (rb0 + b, kcol + p)),
            pl.BlockSpec((dec_seq, LANES), lambda b, p: (rb0 + b, vcol + p)),
            pl.BlockSpec((None, None, n_cache, LANES), lambda b, p: (layer, b, 0, p)),
            pl.BlockSpec((None, None, n_cache, LANES), lambda b, p: (layer, b, 0, p)),
            pl.BlockSpec((None, 2 * dec_seq, n_cache + dec_seq), lambda b, p: (p, 0, 0)),
        ],
        out_specs=pl.BlockSpec((dec_seq, LANES), lambda b, p: (b, p)),
        out_shape=jax.ShapeDtypeStruct((dec_batch * dec_seq, width_a), BF16),
        compiler_params=_cparams(("arbitrary", "arbitrary")),
        name="attn_a_sample",
    )(proj, proj, proj, ck, cv, bias)


def _attn_b_prompt_kernel(q_ref, k_ref, v_ref, bias_ref, sink_ref, o_ref, *, rq, n_heads):
    qb = pl.program_id(2)
    qs = _stack_group_b(q_ref[...] * SCALE, n_heads)
    bias = bias_ref[...]
    prev = qb - 1
    p_start = pl.multiple_of(jnp.maximum(prev, 0) * rq, rq)
    o_start = pl.multiple_of(qb * rq, rq)
    s_p = _nt_dot(qs, k_ref[pl.ds(p_start, rq), :].astype(BF16)) + bias[:, :rq]
    s_p = jnp.where(prev >= 0, s_p, NEG)
    s_o = _nt_dot(qs, k_ref[pl.ds(o_start, rq), :].astype(BF16)) + bias[:, rq:]
    o = _softmax_pv([s_p, s_o],
                    [v_ref[pl.ds(p_start, rq), :].astype(BF16),
                     v_ref[pl.ds(o_start, rq), :].astype(BF16)],
                    sink_ref[...])
    o_ref[...] = _unstack_group_b(o, rq, n_heads).astype(BF16)


def _attn_b_prompt(proj, bias, sink, batch, seq, qcol0, kcol0, vcol0, width_b, n_kv_pairs, rq):
    n_heads = width_b // HEAD_DIM // n_kv_pairs
    qw = n_heads * HEAD_DIM
    nqb = seq // rq
    qc, kc, vc = qcol0 // qw, kcol0 // LANES, vcol0 // LANES
    return pl.pallas_call(
        functools.partial(_attn_b_prompt_kernel, rq=rq, n_heads=n_heads),
        grid=(batch, n_kv_pairs, nqb),
        in_specs=[
            pl.BlockSpec((rq, qw), lambda b, p, i: (b * nqb + i, qc + p)),
            pl.BlockSpec((seq, LANES), lambda b, p, i: (b, kc + p)),
            pl.BlockSpec((seq, LANES), lambda b, p, i: (b, vc + p)),
            pl.BlockSpec((None, n_heads * rq, 2 * rq), lambda b, p, i: (p, 0, 0)),
            pl.BlockSpec((None, n_heads * rq, 1), lambda b, p, i: (p, 0, 0)),
        ],
        out_specs=pl.BlockSpec((rq, qw), lambda b, p, i: (b * nqb + i, p)),
        out_shape=jax.ShapeDtypeStruct((batch * seq, width_b), BF16),
        compiler_params=_cparams(("arbitrary", "arbitrary", "arbitrary")),
        name="attn_b_prompt",
    )(proj, proj, proj, bias, sink)


def _attn_b_sample_kernel(q_ref, kn_ref, vn_ref, kc_ref, vc_ref, bias_ref, sink_ref, o_ref, *,
                          n_cache, n_heads):
    qs = _stack_group_b(q_ref[...] * SCALE, n_heads)
    bias = bias_ref[...]
    s_c = _nt_dot(qs, kc_ref[...].astype(BF16)) + bias[:, :n_cache]
    s_n = _nt_dot(qs, kn_ref[...].astype(BF16)) + bias[:, n_cache:]
    o = _softmax_pv([s_c, s_n], [vc_ref[...].astype(BF16), vn_ref[...].astype(BF16)],
                    sink_ref[...])
    o_ref[...] = _unstack_group_b(o, q_ref.shape[0], n_heads).astype(BF16)


def _attn_b_sample(proj, cache_k, cache_v, bias, sink, layer, row0, dec_batch, dec_seq,
                   qcol0, kcol0, vcol0, width_b, n_kv_pairs):
    n_heads = width_b // HEAD_DIM // n_kv_pairs
    qw = n_heads * HEAD_DIM
    n_cache = cache_k.shape[2]
    kvw = cache_k.shape[3] * cache_k.shape[4]
    rb0 = row0 // dec_seq
    qc, kc, vc = qcol0 // qw, kcol0 // LANES, vcol0 // LANES
    ck = cache_k.reshape(cache_k.shape[0], dec_batch, n_cache, kvw)
    cv = cache_v.reshape(cache_v.shape[0], dec_batch, n_cache, kvw)
    return pl.pallas_call(
        functools.partial(_attn_b_sample_kernel, n_cache=n_cache, n_heads=n_heads),
        grid=(dec_batch, n_kv_pairs),
        in_specs=[
            pl.BlockSpec((dec_seq, qw), lambda b, p: (rb0 + b, qc + p)),
            pl.BlockSpec((dec_seq, LANES), lambda b, p: (rb0 + b, kc + p)),
            pl.BlockSpec((dec_seq, LANES), lambda b, p: (rb0 + b, vc + p)),
            pl.BlockSpec((None, None, n_cache, LANES), lambda b, p: (layer, b, 0, p)),
            pl.BlockSpec((None, None, n_cache, LANES), lambda b, p: (layer, b, 0, p)),
            pl.BlockSpec((None, n_heads * dec_seq, n_cache + dec_seq), lambda b, p: (p, 0, 0)),
            pl.BlockSpec((None, n_heads * dec_seq, 1), lambda b, p: (p, 0, 0)),
        ],
        out_specs=pl.BlockSpec((dec_seq, qw), lambda b, p: (b, p)),
        out_shape=jax.ShapeDtypeStruct((dec_batch * dec_seq, width_b), BF16),
        compiler_params=_cparams(("arbitrary", "arbitrary")),
        name="attn_b_sample",
    )(proj, proj, proj, ck, cv, bias, sink)


def _outproj_kernel(a_ref, w_ref, b_ref, x_ref, gate_ref, o_ref, wbf_ref):
    @pl.when(pl.program_id(1) == 0)
    def _():
        wbf_ref[...] = w_ref[...].astype(BF16)

    y = jnp.dot(a_ref[...], wbf_ref[...], preferred_element_type=F32) + b_ref[...]
    tm, tn = y.shape
    y = y.reshape(tm // CHUNK, CHUNK, tn) * gate_ref[...]
    o_ref[...] = x_ref[...] + y.reshape(tm, tn)


def _outproj(attn, w_out, b_out, x, modc, gate_col, layer):
    n, kdim = attn.shape
    d = w_out.shape[-1]
    tm = _pick(n, (512, 256, 128))
    tn = _pick(d, (512, 256, 128))
    cpt = tm // CHUNK
    gc = gate_col * (d // tn)
    return pl.pallas_call(
        _outproj_kernel,
        grid=(d // tn, n // tm),
        in_specs=[
            pl.BlockSpec((tm, kdim), lambda j, i: (i, 0)),
            pl.BlockSpec((None, kdim, tn), lambda j, i: (layer, 0, j)),
            pl.BlockSpec((None, 1, tn), lambda j, i: (layer, 0, j)),
            pl.BlockSpec((tm, tn), lambda j, i: (i, j)),
            pl.BlockSpec((cpt, 1, tn), lambda j, i: (i, 0, gc + j)),
        ],
        out_specs=pl.BlockSpec((tm, tn), lambda j, i: (i, j)),
        out_shape=jax.ShapeDtypeStruct((n, d), F32),
        scratch_shapes=[pltpu.VMEM((kdim, tn), BF16)],
        compiler_params=_cparams(("arbitrary", "arbitrary")),
        name="out_proj",
    )(attn, w_out, b_out, x, modc)


def _router_kernel(x_ref, g_ref, scale_ref, shift_ref, wr_ref, br_ref,
                   hp_ref, e_ref, gate_ref, rank_ref, cnt_ref, whi_ref, wlo_ref, *, n_experts):
    @pl.when(pl.program_id(0) == 0)
    def _():
        w = wr_ref[...]
        hi = w.astype(BF16)
        whi_ref[...] = hi
        wlo_ref[...] = (w - hi.astype(F32)).astype(BF16)

    h = _rms_mod(x_ref[...], g_ref[...], scale_ref[...], shift_ref[...])
    tm, d = h.shape
    hi = h.astype(BF16)
    hi32 = hi.astype(F32)
    lo = (h - hi32).astype(BF16)
    bits = lax.bitcast_convert_type(hi32, jnp.uint32)
    hp_ref[...] = bits[:, :d // 2] | (bits[:, d // 2:] >> 16)

    logits = (jnp.dot(hi, whi_ref[...], preferred_element_type=F32)
              + jnp.dot(lo, whi_ref[...], preferred_element_type=F32)
              + jnp.dot(hi, wlo_ref[...], preferred_element_type=F32)) + br_ref[...]

    lane_e = lax.broadcasted_iota(jnp.int32, (tm, n_experts), 1).astype(F32)
    lane_o = lax.broadcasted_iota(jnp.int32, (tm, LANES), 1)
    work = logits
    tops, idxs = [], []
    for _ in range(TOP_K):
        mx = jnp.max(work, axis=-1, keepdims=True)
        ix = jnp.min(jnp.where(work == mx, lane_e, float(n_experts)), axis=-1, keepdims=True)
        tops.append(mx)
        idxs.append(ix)
        work = jnp.where(lane_e == ix, -jnp.inf, work)
    exps = [jnp.exp(t - tops[0]) for t in tops]
    den = exps[0] + exps[1] + exps[2] + exps[3]
    e_out = jnp.zeros((tm, LANES), F32)
    g_out = jnp.zeros((tm, LANES), F32)
    for k in range(TOP_K):
        e_out = jnp.where(lane_o == k, idxs[k], e_out)
        g_out = jnp.where(lane_o == k, exps[k] / den, g_out)
    e_ref[...] = e_out.astype(jnp.int32)
    gate_ref[...] = g_out

    @pl.when(pl.program_id(0) == 0)
    def _():
        cnt_ref[...] = jnp.zeros_like(cnt_ref)

    lane_f = lane_o.astype(F32)
    sel = jnp.zeros((tm, LANES), F32)
    for k in range(TOP_K):
        sel = sel + (lane_f == idxs[k]).astype(F32)
    tri = (lax.broadcasted_iota(jnp.int32, (tm, tm), 0)
           > lax.broadcasted_iota(jnp.int32, (tm, tm), 1)).astype(BF16)
    before = jnp.dot(tri, sel.astype(BF16), preferred_element_type=F32) + cnt_ref[0:1, :]
    r_out = jnp.zeros((tm, LANES), F32)
    for k in range(TOP_K):
        rk = jnp.sum(jnp.where(lane_f == idxs[k], before, 0.0), axis=-1, keepdims=True)
        r_out = jnp.where(lane_o == k, rk, r_out)
    rank_ref[...] = r_out.astype(jnp.int32)
    cnt_ref[...] = cnt_ref[...] + jnp.sum(sel, axis=0, keepdims=True)


def _router(x, g_all, layer, modc, scale_col, shift_col, w_router, b_router):
    n, d = x.shape
    n_experts = w_router.shape[-1]
    tm = _pick(n, (256, 128, 64))
    cpt = tm // CHUNK
    depth = w_router.shape[0]
    return pl.pallas_call(
        functools.partial(_router_kernel, n_experts=n_experts),
        grid=(n // tm,),
        in_specs=[
            pl.BlockSpec((tm, d), lambda i: (i, 0)),
            pl.BlockSpec((None, 1, d), lambda i: (layer, 0, 0)),
            pl.BlockSpec((cpt, 1, d), lambda i: (i, 0, scale_col)),
            pl.BlockSpec((cpt, 1, d), lambda i: (i, 0, shift_col)),
            pl.BlockSpec((None, d, n_experts), lambda i: (layer, 0, 0)),
            pl.BlockSpec((None, 1, n_experts), lambda i: (layer, 0, 0)),
        ],
        out_specs=[
            pl.BlockSpec((tm, d // 2), lambda i: (i, 0)),
            pl.BlockSpec((tm, LANES), lambda i: (i, 0)),
            pl.BlockSpec((tm, LANES), lambda i: (i, 0)),
            pl.BlockSpec((tm, LANES), lambda i: (i, 0)),
            pl.BlockSpec((8, LANES), lambda i: (0, 0)),
        ],
        out_shape=[
            jax.ShapeDtypeStruct((n, d // 2), jnp.uint32),
            jax.ShapeDtypeStruct((n, LANES), jnp.int32),
            jax.ShapeDtypeStruct((n, LANES), F32),
            jax.ShapeDtypeStruct((n, LANES), jnp.int32),
            jax.ShapeDtypeStruct((8, LANES), F32),
        ],
        scratch_shapes=[pltpu.VMEM((d, n_experts), BF16), pltpu.VMEM((d, n_experts), BF16)],
        compiler_params=_cparams(("arbitrary",)),
        name="norm_router",
    )(x, g_all, modc, modc, w_router, b_router.reshape(depth, 1, n_experts))


DMA_ISSUE_UNROLL = 8
BULK_DMA_PRIORITY = 1


def _row_copy(src, dst, sem, src_row, dst_row):
    return pltpu.make_async_copy(src.at[pl.ds(src_row, 1)], dst.at[pl.ds(dst_row, 1)], sem)


def _tile_wait(src, dst, sem, tm):
    pltpu.make_async_copy(src.at[pl.ds(0, tm)], dst.at[pl.ds(0, tm)], sem).wait()


def _gather_kernel(tok_ref, h_hbm, o_hbm, buf_ref, in_sem, out_sem, *, tm):
    i = pl.program_id(0)
    n_steps = pl.num_programs(0)
    slot = i % 2

    def issue_tile(tile, s):
        base = tile * tm

        def issue(r, carry):
            _row_copy(h_hbm, buf_ref.at[s], in_sem.at[s], tok_ref[base + r], r).start()
            return carry

        lax.fori_loop(0, tm, issue, 0, unroll=DMA_ISSUE_UNROLL)

    def out_copy(tile, s):
        rows = pl.ds(pl.multiple_of(tile * tm, tm), tm)
        return pltpu.make_async_copy(buf_ref.at[s], o_hbm.at[rows], out_sem.at[s])

    @pl.when(i == 0)
    def _():
        issue_tile(0, 0)

    @pl.when(i >= 1)
    def _():
        out_copy(i - 1, 1 - slot).wait()

    @pl.when(i + 1 < n_steps)
    def _():
        issue_tile(i + 1, 1 - slot)

    _tile_wait(h_hbm, buf_ref.at[slot], in_sem.at[slot], tm)
    out_copy(i, slot).start(priority=BULK_DMA_PRIORITY)

    @pl.when(i == n_steps - 1)
    def _():
        out_copy(i, slot).wait()


def _gather_rows(row_tok, hp, tm):
    n_rows = row_tok.shape[0]
    grid_spec = pltpu.PrefetchScalarGridSpec(
        num_scalar_prefetch=1,
        grid=(n_rows // tm,),
        in_specs=[pl.BlockSpec(memory_space=pl.ANY)],
        out_specs=pl.BlockSpec(memory_space=pl.ANY),
        scratch_shapes=[pltpu.VMEM((2, tm, hp.shape[1]), hp.dtype),
                        pltpu.SemaphoreType.DMA((2,)), pltpu.SemaphoreType.DMA((2,))],
    )
    return pl.pallas_call(
        functools.partial(_gather_kernel, tm=tm),
        grid_spec=grid_spec,
        out_shape=jax.ShapeDtypeStruct((n_rows, hp.shape[1]), hp.dtype),
        compiler_params=_cparams(("arbitrary",)),
        name="moe_gather",
    )(row_tok, hp)


def _unpack_rows(words):
    first = lax.bitcast_convert_type(words & jnp.uint32(0xFFFF0000), F32).astype(BF16)
    second = lax.bitcast_convert_type(words << 16, F32).astype(BF16)
    return first, second


def _expert_tile_loop(ts_ref, tc_ref, nu_ref, src_hbm, dst_hbm, ibuf, obuf, in_sem, out_sem,
                      compute, *, tm, tn):
    j, e = pl.program_id(0), pl.program_id(1)
    first, count = ts_ref[e], tc_ref[e]
    cols = pl.ds(pl.multiple_of(j * tn, tn), tn)

    def rows(tile):
        return pl.ds(pl.multiple_of(tile * tm, tm), tm)

    def load(t, s):
        return pltpu.make_async_copy(src_hbm.at[rows(first + t)], ibuf.at[s], in_sem.at[s])

    def store(tile, s):
        return pltpu.make_async_copy(obuf.at[s], dst_hbm.at[rows(tile), cols], out_sem.at[s])

    @pl.when(count > 0)
    def _():
        load(0, 0).start(priority=BULK_DMA_PRIORITY)

    def body(t, carry):
        s = t % 2
        load(t, s).wait()

        @pl.when(t + 1 < count)
        def _():
            load(t + 1, 1 - s).start(priority=BULK_DMA_PRIORITY)

        @pl.when(t >= 2)
        def _():
            store(first + t - 2, s).wait()

        obuf[s] = compute(ibuf[s])
        store(first + t, s).start(priority=BULK_DMA_PRIORITY)
        return carry

    lax.fori_loop(0, count, body, 0)

    @pl.when(count >= 2)
    def _():
        store(first + count - 2, count % 2).wait()

    @pl.when(count >= 1)
    def _():
        store(first + count - 1, (count - 1) % 2).wait()

    @pl.when(e == pl.num_programs(1) - 1)
    def _():
        obuf[0] = jnp.zeros(obuf.shape[1:], obuf.dtype)

        def fill(tile, carry):
            cp = store(tile, 0)
            cp.start()
            cp.wait()
            return carry

        lax.fori_loop(nu_ref[0], dst_hbm.shape[0] // tm, fill, 0)


def _up_kernel(ts_ref, tc_ref, nu_ref, x_hbm, wg_ref, wl_ref, bg_ref, bl_ref, o_hbm,
               xbuf, obuf, wgb_ref, wlb_ref, in_sem, out_sem, *, tm, tn):
    wgb_ref[...] = wg_ref[...].astype(BF16)
    wlb_ref[...] = wl_ref[...].astype(BF16)

    def compute(words):
        x0, x1 = _unpack_rows(words)
        half = x0.shape[1]
        glu = (jnp.dot(x0, wgb_ref[:half, :], preferred_element_type=F32)
               + jnp.dot(x1, wgb_ref[half:, :], preferred_element_type=F32)) + bg_ref[...]
        lin = (jnp.dot(x0, wlb_ref[:half, :], preferred_element_type=F32)
               + jnp.dot(x1, wlb_ref[half:, :], preferred_element_type=F32)) + bl_ref[...]
        glu = jnp.minimum(glu, SWIGLU_LIMIT)
        lin = jnp.clip(lin, -SWIGLU_LIMIT, SWIGLU_LIMIT)
        act = glu * (1.0 / (1.0 + jnp.exp(-SWIGLU_ALPHA * glu))) * (lin + 1.0)
        return act.astype(BF16)

    _expert_tile_loop(ts_ref, tc_ref, nu_ref, x_hbm, o_hbm, xbuf, obuf, in_sem, out_sem, compute,
                      tm=tm, tn=tn)


def _expert_up(tile_first, tile_count, n_used, xs, w_up, b_up, layer, tm):
    depth, n_experts, d, two_ff = w_up.shape
    n_rows, half = xs.shape
    d_ff = two_ff // 2
    tn = _pick(d_ff, (512, 256, 128))
    nj = d_ff // tn
    b4 = b_up.reshape(depth, n_experts, 1, two_ff)
    grid_spec = pltpu.PrefetchScalarGridSpec(
        num_scalar_prefetch=3,
        grid=(nj, n_experts),
        in_specs=[
            pl.BlockSpec(memory_space=pl.ANY),
            pl.BlockSpec((None, None, d, tn), lambda j, e, *_: (layer, e, 0, j)),
            pl.BlockSpec((None, None, d, tn), lambda j, e, *_: (layer, e, 0, nj + j)),
            pl.BlockSpec((None, None, 1, tn), lambda j, e, *_: (layer, e, 0, j)),
            pl.BlockSpec((None, None, 1, tn), lambda j, e, *_: (layer, e, 0, nj + j)),
        ],
        out_specs=pl.BlockSpec(memory_space=pl.ANY),
        scratch_shapes=[pltpu.VMEM((2, tm, half), xs.dtype), pltpu.VMEM((2, tm, tn), BF16),
                        pltpu.VMEM((d, tn), BF16), pltpu.VMEM((d, tn), BF16),
                        pltpu.SemaphoreType.DMA((2,)), pltpu.SemaphoreType.DMA((2,))],
    )
    return pl.pallas_call(
        functools.partial(_up_kernel, tm=tm, tn=tn),
        grid_spec=grid_spec,
        out_shape=jax.ShapeDtypeStruct((n_rows, d_ff), BF16),
        compiler_params=_cparams(("arbitrary", "arbitrary")),
        name="expert_up",
    )(tile_first, tile_count, n_used, xs, w_up, w_up, b4, b4)


def _down_kernel(ts_ref, tc_ref, nu_ref, a_hbm, w_ref, b_ref, o_hbm,
                 abuf, obuf, wb_ref, in_sem, out_sem, *, tm, tn):
    wb_ref[...] = w_ref[...].astype(BF16)

    def compute(a):
        return jnp.dot(a, wb_ref[...], preferred_element_type=F32) + b_ref[...]

    _expert_tile_loop(ts_ref, tc_ref, nu_ref, a_hbm, o_hbm, abuf, obuf, in_sem, out_sem, compute,
                      tm=tm, tn=tn)


def _expert_down(tile_first, tile_count, n_used, act, w_down, b_down, layer, tm):
    n_rows, d_ff = act.shape
    depth, n_experts, _, d = w_down.shape
    tn = _pick(d, (2048, 1024, 512, 256, 128))
    b4 = b_down.reshape(depth, n_experts, 1, d)
    grid_spec = pltpu.PrefetchScalarGridSpec(
        num_scalar_prefetch=3,
        grid=(d // tn, n_experts),
        in_specs=[
            pl.BlockSpec(memory_space=pl.ANY),
            pl.BlockSpec((None, None, d_ff, tn), lambda j, e, *_: (layer, e, 0, j)),
            pl.BlockSpec((None, None, 1, tn), lambda j, e, *_: (layer, e, 0, j)),
        ],
        out_specs=pl.BlockSpec(memory_space=pl.ANY),
        scratch_shapes=[pltpu.VMEM((2, tm, d_ff), act.dtype), pltpu.VMEM((2, tm, tn), F32),
                        pltpu.VMEM((d_ff, tn), BF16),
                        pltpu.SemaphoreType.DMA((2,)), pltpu.SemaphoreType.DMA((2,))],
    )
    return pl.pallas_call(
        functools.partial(_down_kernel, tm=tm, tn=tn),
        grid_spec=grid_spec,
        out_shape=jax.ShapeDtypeStruct((n_rows, d), F32),
        compiler_params=_cparams(("arbitrary", "arbitrary")),
        name="expert_down",
    )(tile_first, tile_count, n_used, act, w_down, b4)


def _combine_kernel(pos_ref, gate_ref, x_ref, g2_ref, y_hbm, o_ref, buf_ref, sem, *, tm):
    i = pl.program_id(0)
    n_steps = pl.num_programs(0)

    def issue_tile(tile, slot):
        base = tile * (tm * TOP_K)

        def issue(r, carry):
            for k in range(TOP_K):
                _row_copy(y_hbm, buf_ref.at[slot, k], sem.at[slot],
                          pos_ref[base + r * TOP_K + k], r).start(priority=BULK_DMA_PRIORITY)
            return carry

        lax.fori_loop(0, tm, issue, 0, unroll=DMA_ISSUE_UNROLL // TOP_K)

    slot = i % 2

    @pl.when(i == 0)
    def _():
        issue_tile(0, 0)

    @pl.when(i + 1 < n_steps)
    def _():
        issue_tile(i + 1, 1 - slot)

    for k in range(TOP_K):
        _tile_wait(y_hbm, buf_ref.at[slot, k], sem.at[slot], tm)

    gate = gate_ref[...]
    acc = gate[:, 0:1] * buf_ref[slot, 0]
    for k in range(1, TOP_K):
        acc = acc + gate[:, k:k + 1] * buf_ref[slot, k]
    d = acc.shape[1]
    acc = acc.reshape(tm // CHUNK, CHUNK, d) * g2_ref[...]
    o_ref[...] = x_ref[...] + acc.reshape(tm, d)


def _combine(pos, gate, x, modc, gate_col, yb):
    n, d = x.shape
    tm = _pick(n, (128, 64))
    cpt = tm // CHUNK
    grid_spec = pltpu.PrefetchScalarGridSpec(
        num_scalar_prefetch=1,
        grid=(n // tm,),
        in_specs=[
            pl.BlockSpec((tm, LANES), lambda i, p: (i, 0)),
            pl.BlockSpec((tm, d), lambda i, p: (i, 0)),
            pl.BlockSpec((cpt, 1, d), lambda i, p: (i, 0, gate_col)),
            pl.BlockSpec(memory_space=pl.ANY),
        ],
        out_specs=pl.BlockSpec((tm, d), lambda i, p: (i, 0)),
        scratch_shapes=[pltpu.VMEM((2, TOP_K, tm, d), F32), pltpu.SemaphoreType.DMA((2,))],
    )
    return pl.pallas_call(
        functools.partial(_combine_kernel, tm=tm),
        grid_spec=grid_spec,
        out_shape=jax.ShapeDtypeStruct((n, d), F32),
        compiler_params=_cparams(("arbitrary",)),
        name="moe_combine",
    )(pos, gate, x, modc, yb)


def _dispatch_plan(top_e, rank, counts, n_experts, tm):
    n_tok = top_e.shape[0]
    n_assign = n_tok * TOP_K
    flat_e = top_e.reshape(n_assign)
    rank = rank.reshape(n_assign)
    padded = (counts + tm - 1) // tm * tm
    p_end = jnp.cumsum(padded)
    p_start = p_end - padded
    dest = (p_start[flat_e] + rank).astype(jnp.int32)
    n_tiles = n_assign // tm + n_experts
    row_tok = jnp.zeros((n_tiles * tm,), jnp.int32).at[dest].set(
        jnp.arange(n_assign, dtype=jnp.int32) // TOP_K)
    tile_first = (p_start // tm).astype(jnp.int32)
    tile_count = (padded // tm).astype(jnp.int32)
    n_used = (p_end[-1] // tm).astype(jnp.int32).reshape(1)
    return dest, row_tok, tile_first, tile_count, n_used


def _bias_a(rel_bias, rq, n_keys, masked):
    n_heads = rel_bias.shape[0]
    qi = jnp.arange(rq)[:, None]
    ki = jnp.arange(n_keys)[None, :]
    off = n_keys - rq
    n_diag = n_keys + rq - 1
    rel = jnp.clip(jnp.arange(n_diag) - (rq - 1) - off, -REL_CLIP, REL_CLIP) + REL_CLIP
    ext = jnp.pad(jnp.take(rel_bias, rel, axis=1), ((0, 0), (0, 1)))
    skew = jnp.tile(ext, (1, rq))[:, :rq * n_diag].reshape(n_heads, rq, n_diag)
    bias = skew[:, :, rq - 1:rq - 1 + n_keys]
    if masked:
        qc = (qi + off) // CHUNK
        kc = ki // CHUNK
        ok = (kc <= qc) & (kc >= qc - BAND_CHUNKS_A)
        bias = jnp.where(ok[None], bias, NEG)
    return bias.reshape(n_heads // 2, 2 * rq, n_keys)


def _bias_b(n_heads_total, n_kv_pairs, rq, n_keys, masked):
    slopes = 2.0 ** (-8.0 * jnp.arange(1, n_heads_total + 1, dtype=F32) / n_heads_total)
    qi = jnp.arange(rq)[:, None]
    ki = jnp.arange(n_keys)[None, :]
    off = n_keys - rq
    dist = jnp.abs(qi - (ki - off)).astype(F32)
    bias = -slopes[:, None, None] * dist[None]
    if masked:
        qc = (qi + off) // CHUNK
        kc = ki // CHUNK
        ok = (kc <= qc) & (kc >= qc - BAND_CHUNKS_B)
        bias = jnp.where(ok[None], bias, NEG)
    return bias.reshape(n_kv_pairs, (n_heads_total // n_kv_pairs) * rq, n_keys)


def kernel(x_prompt, x_sample, c_prompt, c_sample, cache_k_a, cache_v_a, cache_k_b, cache_v_b,
           w_ada, b_ada, g_attn, g_ffn, w_in, b_in, g_q_a, g_k_a, g_q_b, g_k_b, rel_bias_a,
           sinks_b, w_out, b_out, w_router, b_router, w_up, b_up, w_down, b_down):
    batch, seq, d = x_prompt.shape
    dec_batch, dec_seq, _ = x_sample.shape
    depth = w_in.shape[0]
    heads_a = rel_bias_a.shape[1]
    heads_b = sinks_b.shape[1]
    kv_heads_b = cache_k_b.shape[3]
    n_experts = w_router.shape[-1]
    width_a, width_b, kv_width_b = heads_a * HEAD_DIM, heads_b * HEAD_DIM, kv_heads_b * HEAD_DIM
    in_width = w_in.shape[-1]
    n_cache_a, n_cache_b = cache_k_a.shape[2], cache_k_b.shape[2]
    band_a, band_b = BAND_CHUNKS_A * CHUNK, BAND_CHUNKS_B * CHUNK
    assert dec_seq == CHUNK and PAST_LEN % CHUNK == 0 and seq % CHUNK == 0
    assert n_cache_a == band_a and n_cache_b == band_b
    assert in_width == 3 * width_a + width_b + 2 * kv_width_b and width_a + width_b == d
    assert kv_heads_b % 2 == 0 and heads_a % 2 == 0

    n_p, n_s = batch * seq, dec_batch * dec_seq
    n_tok = n_p + n_s
    x = jnp.concatenate([x_prompt.reshape(n_p, d), x_sample.reshape(n_s, d)], axis=0)

    n_seq = batch + dec_batch
    c_all = jnp.concatenate([c_prompt, c_sample], axis=0)
    c_pad = jnp.pad(c_all, ((0, -n_seq % 8), (0, 0)))
    mod_all = _ada(c_pad, w_ada, b_ada)
    shift1_c, scale1_c, gate1_c, shift2_c, scale2_c, gate2_c = range(N_MOD)

    seg = (width_a, 2 * width_a, 3 * width_a, 3 * width_a + width_b,
           3 * width_a + width_b + kv_width_b, in_width)
    tn_in = _pick(math.gcd(*seg), (512, 256, 128))
    col = jnp.arange(in_width)
    is_v = ((col >= seg[1]) & (col < seg[2])) | (col >= seg[4])
    flags = jnp.logical_not(is_v[::tn_in]).astype(jnp.int32)

    def gains(l):
        ones_a, ones_kv = jnp.ones((width_a,), F32), jnp.ones((kv_width_b,), F32)
        return jnp.concatenate([jnp.tile(g_q_a[l], heads_a), jnp.tile(g_k_a[l], heads_a), ones_a,
                                jnp.tile(g_q_b[l], heads_b), jnp.tile(g_k_b[l], kv_heads_b),
                                ones_kv])

    gain_all = jnp.stack([gains(l) for l in range(depth)]).reshape(depth, 1, in_width)
    b_in3 = b_in.reshape(depth, 1, in_width)
    b_out3 = b_out.reshape(depth, 1, d)
    g_attn3 = g_attn.reshape(depth, 1, d)
    g_ffn3 = g_ffn.reshape(depth, 1, d)

    rq_a = 4 * CHUNK
    rq_b = 2 * CHUNK
    n_kv_pairs = kv_heads_b // 2
    heads_per_pair = heads_b // n_kv_pairs
    bias_bp = _bias_b(heads_b, n_kv_pairs, rq_b, 2 * rq_b, True)
    bias_bs = _bias_b(heads_b, n_kv_pairs, dec_seq, n_cache_b + dec_seq, False)

    tm_moe = 256
    new_ka_p, new_va_p, new_kb_p, new_vb_p = [], [], [], []
    new_ka_s, new_va_s, new_kb_s, new_vb_s = [], [], [], []
    for l in range(depth):
        mod_p = jnp.broadcast_to(mod_all[l, :batch, None, :], (batch, seq // CHUNK, N_MOD * d))
        modc = jnp.concatenate([mod_p.reshape(n_p // CHUNK, N_MOD * d), mod_all[l, batch:n_seq]],
                               axis=0).reshape(n_tok // CHUNK, 1, N_MOD * d)
        h = _norm(x, g_attn3, l, modc, scale1_c, shift1_c)
        proj = _inproj(h, w_in, b_in3, gain_all, flags, l, tn_in)

        bias_ap = _bias_a(rel_bias_a[l], rq_a, band_a + rq_a, True)
        bias_ap = bias_ap.reshape(heads_a // 2, 2 * rq_a, -1, rq_a).transpose(0, 2, 1, 3)
        bias_as = _bias_a(rel_bias_a[l], dec_seq, n_cache_a + dec_seq, False)
        sink_p = jnp.repeat(sinks_b[l], rq_b).reshape(n_kv_pairs, heads_per_pair * rq_b, 1)
        sink_s = jnp.repeat(sinks_b[l], dec_seq).reshape(n_kv_pairs, heads_per_pair * dec_seq, 1)

        oa_p = _attn_a_prompt(proj, bias_ap, batch, seq, width_a, rq_a)
        oa_s = _attn_a_sample(proj, cache_k_a, cache_v_a, bias_as, l, n_p, dec_batch, dec_seq,
                              width_a)
        ob_p = _attn_b_prompt(proj, bias_bp, sink_p, batch, seq, seg[2], seg[3], seg[4],
                              width_b, n_kv_pairs, rq_b)
        ob_s = _attn_b_sample(proj, cache_k_b, cache_v_b, bias_bs, sink_s, l, n_p, dec_batch,
                              dec_seq, seg[2], seg[3], seg[4], width_b, n_kv_pairs)
        attn = jnp.concatenate([jnp.concatenate([oa_p, ob_p], axis=1),
                                jnp.concatenate([oa_s, ob_s], axis=1)], axis=0)
        x = _outproj(attn, w_out, b_out3, x, modc, gate1_c, l)

        hp, e_pad, gate_pad, rank_pad, cnt = _router(x, g_ffn3, l, modc, scale2_c, shift2_c,
                                                     w_router, b_router)
        dest, row_tok, tile_first, tile_count, n_used = _dispatch_plan(
            e_pad[:, :TOP_K], rank_pad[:, :TOP_K], cnt[0, :n_experts].astype(jnp.int32),
            n_experts, tm_moe)
        xs = _gather_rows(row_tok, hp, tm_moe)
        act = _expert_up(tile_first, tile_count, n_used, xs, w_up, b_up, l, tm_moe)
        yb = _expert_down(tile_first, tile_count, n_used, act, w_down, b_down, l, tm_moe)
        x = _combine(dest, gate_pad, x, modc, gate2_c, yb)

        ka = proj[:, seg[0]:seg[1]]
        va = proj[:, seg[1]:seg[2]]
        kb = proj[:, seg[3]:seg[4]]
        vb = proj[:, seg[4]:seg[5]]
        for dst, src, nh, keep in ((new_ka_p, ka, heads_a, band_a), (new_va_p, va, heads_a, band_a),
                                   (new_kb_p, kb, kv_heads_b, band_b),
                                   (new_vb_p, vb, kv_heads_b, band_b)):
            last = jnp.stack([src[(b + 1) * seq - keep:(b + 1) * seq] for b in range(batch)])
            dst.append(last.reshape(batch, keep, nh, HEAD_DIM))
        for dst, src, nh in ((new_ka_s, ka, heads_a), (new_va_s, va, heads_a),
                             (new_kb_s, kb, kv_heads_b), (new_vb_s, vb, kv_heads_b)):
            dst.append(src[n_p:].reshape(dec_batch, dec_seq, nh, HEAD_DIM))

    return (x[:n_p].reshape(batch, seq, d), x[n_p:].reshape(dec_batch, dec_seq, d),
            jnp.stack(new_ka_p), jnp.stack(new_va_p), jnp.stack(new_kb_p), jnp.stack(new_vb_p),
            jnp.stack(new_ka_s), jnp.stack(new_va_s), jnp.stack(new_kb_s), jnp.stack(new_vb_s))
```

```python
import functools
import math

import jax
import jax.numpy as jnp
from jax import lax
from jax.experimental import pallas as pl
from jax.experimental.pallas import tpu as pltpu

CHUNK = 64
HEAD_DIM = 64
LANES = 128
MXU_WIDTH = 256
TOP_K = 4
BAND_CHUNKS_A = 8
BAND_CHUNKS_B = 2
REL_CLIP = 128
PAST_LEN = 1024
RMS_EPS = 1e-5
SWIGLU_ALPHA = 1.702
SWIGLU_LIMIT = 7.0
SCALE = HEAD_DIM ** -0.5
NEG = -1e30
N_MOD = 6
VMEM_LIMIT = 56 * 1024 * 1024

F32 = jnp.float32
BF16 = jnp.bfloat16


def _cparams(sem):
    return pltpu.CompilerParams(dimension_semantics=sem, vmem_limit_bytes=VMEM_LIMIT)


def _pick(n, cands):
    for c in cands:
        if n % c == 0:
            return c
    raise ValueError(f"no tile in {cands} divides {n}")


def _nt_dot(a, b):
    return lax.dot_general(a, b, (((1,), (1,)), ((), ())), preferred_element_type=F32)


def _ada_kernel(c_ref, w_ref, b_ref, o_ref):
    c = c_ref[...]
    s = (c * (1.0 / (1.0 + jnp.exp(-c)))).astype(BF16)
    o_ref[...] = jnp.dot(s, w_ref[...].astype(BF16), preferred_element_type=F32) + b_ref[...]


def _ada(c_pad, w_ada, b_ada):
    depth, d, n = w_ada.shape
    rows = c_pad.shape[0]
    tn = _pick(n, (512, 256, 128))
    return pl.pallas_call(
        _ada_kernel,
        grid=(depth, n // tn),
        in_specs=[
            pl.BlockSpec((rows, d), lambda l, j: (0, 0)),
            pl.BlockSpec((None, d, tn), lambda l, j: (l, 0, j)),
            pl.BlockSpec((None, 1, tn), lambda l, j: (l, 0, j)),
        ],
        out_specs=pl.BlockSpec((None, rows, tn), lambda l, j: (l, 0, j)),
        out_shape=jax.ShapeDtypeStruct((depth, rows, n), F32),
        compiler_params=_cparams(("arbitrary", "arbitrary")),
        name="ada_mod",
    )(c_pad, w_ada, b_ada.reshape(depth, 1, n))


def _rms_mod(x, g, scale, shift):
    tm, d = x.shape
    ms = jnp.mean(x * x, axis=-1, keepdims=True)
    y = x * lax.rsqrt(ms + RMS_EPS) * g
    y = y.reshape(tm // CHUNK, CHUNK, d)
    return (y * (1.0 + scale) + shift).reshape(tm, d)


def _norm_kernel(x_ref, g_ref, scale_ref, shift_ref, h_ref):
    h_ref[...] = _rms_mod(x_ref[...], g_ref[...], scale_ref[...], shift_ref[...]).astype(BF16)


def _norm(x, g_all, layer, modc, scale_col, shift_col):
    n, d = x.shape
    tm = _pick(n, (256, 128, 64))
    cpt = tm // CHUNK
    return pl.pallas_call(
        _norm_kernel,
        grid=(n // tm,),
        in_specs=[
            pl.BlockSpec((tm, d), lambda i: (i, 0)),
            pl.BlockSpec((None, 1, d), lambda i: (layer, 0, 0)),
            pl.BlockSpec((cpt, 1, d), lambda i: (i, 0, scale_col)),
            pl.BlockSpec((cpt, 1, d), lambda i: (i, 0, shift_col)),
        ],
        out_specs=pl.BlockSpec((tm, d), lambda i: (i, 0)),
        out_shape=jax.ShapeDtypeStruct((n, d), BF16),
        compiler_params=_cparams(("arbitrary",)),
        name="norm_mod",
    )(x, g_all, modc, modc)


def _inproj_kernel(flag_ref, h_ref, w_ref, b_ref, gain_ref, grp_ref, o_ref, wbf_ref):
    j = pl.program_id(0)

    @pl.when(pl.program_id(1) == 0)
    def _():
        wbf_ref[...] = w_ref[...].astype(BF16)

    y = jnp.dot(h_ref[...], wbf_ref[...], preferred_element_type=F32) + b_ref[...]

    @pl.when(flag_ref[j] == 0)
    def _():
        o_ref[...] = y

    @pl.when(flag_ref[j] != 0)
    def _():
        grp = grp_ref[...]
        gw = grp.shape[0]
        for c in range(y.shape[1] // gw):
            yc = y[:, c * gw:(c + 1) * gw]
            ss = jnp.dot((yc * yc).astype(BF16), grp, preferred_element_type=F32)
            o_ref[:, c * gw:(c + 1) * gw] = (yc * lax.rsqrt(ss * (1.0 / HEAD_DIM) + RMS_EPS)
                                             * gain_ref[:, c * gw:(c + 1) * gw])


def _inproj(h, w_in, b_in, gain, flags, layer, tn):
    n, d = h.shape
    width = w_in.shape[-1]
    tm = _pick(n, (512, 256, 128))
    gw = min(tn, MXU_WIDTH)
    grp = (jnp.arange(gw)[:, None] // HEAD_DIM == jnp.arange(gw)[None, :] // HEAD_DIM).astype(BF16)
    grid_spec = pltpu.PrefetchScalarGridSpec(
        num_scalar_prefetch=1,
        grid=(width // tn, n // tm),
        in_specs=[
            pl.BlockSpec((tm, d), lambda j, i, f: (i, 0)),
            pl.BlockSpec((None, d, tn), lambda j, i, f: (layer, 0, j)),
            pl.BlockSpec((None, 1, tn), lambda j, i, f: (layer, 0, j)),
            pl.BlockSpec((None, 1, tn), lambda j, i, f: (layer, 0, j)),
            pl.BlockSpec((gw, gw), lambda j, i, f: (0, 0)),
        ],
        out_specs=pl.BlockSpec((tm, tn), lambda j, i, f: (i, j)),
        scratch_shapes=[pltpu.VMEM((d, tn), BF16)],
    )
    return pl.pallas_call(
        _inproj_kernel,
        grid_spec=grid_spec,
        out_shape=jax.ShapeDtypeStruct((n, width), F32),
        compiler_params=_cparams(("arbitrary", "arbitrary")),
        name="in_proj",
    )(flags, h, w_in, b_in, gain, grp)


def _softmax_pv(scores, values, sink=None, *, row_parts=1, mxu_sum=False):
    if row_parts > 1:
        step = scores[0].shape[0] // row_parts
        outs = []
        for i in range(row_parts):
            rows = slice(i * step, (i + 1) * step)
            outs.append(_softmax_pv([s[rows] for s in scores], values,
                                    None if sink is None else sink[rows], mxu_sum=mxu_sum))
        return jnp.concatenate(outs, axis=0)
    m = None
    for s in scores:
        mj = jnp.max(s, axis=-1, keepdims=True)
        m = mj if m is None else jnp.maximum(m, mj)
    if sink is not None:
        m = jnp.maximum(m, sink)
    den = None
    out = None
    for s, v in zip(scores, values):
        p = jnp.exp(s - m)
        pb = p.astype(BF16)
        if mxu_sum:
            ones = jnp.ones((pb.shape[1], LANES), BF16)
            dj = jnp.dot(pb, ones, preferred_element_type=F32)[:, :1]
        else:
            dj = jnp.sum(p, axis=-1, keepdims=True)
        oj = jnp.dot(pb, v, preferred_element_type=F32)
        den = dj if den is None else den + dj
        out = oj if out is None else out + oj
    if sink is not None:
        den = den + jnp.exp(sink - m)
    return out / den


def _stack_pair_a(q):
    lo = lax.broadcasted_iota(jnp.int32, q.shape, 1) < HEAD_DIM
    return jnp.concatenate([jnp.where(lo, q, 0.0), jnp.where(lo, 0.0, q)], axis=0).astype(BF16)


def _unstack_pair_a(o, rows):
    lo = lax.broadcasted_iota(jnp.int32, (rows, LANES), 1) < HEAD_DIM
    return jnp.where(lo, o[:rows], o[rows:])


def _stack_group_b(q, n_heads):
    rows = q.shape[0]
    lo = lax.broadcasted_iota(jnp.int32, (rows, LANES), 1) < HEAD_DIM
    per_kv = n_heads // 2
    parts = []
    for i in range(n_heads):
        blk = q[:, (i // 2) * LANES:(i // 2 + 1) * LANES]
        src_half, kv_half = i % 2, i // per_kv
        if src_half != kv_half:
            blk = pltpu.roll(blk, HEAD_DIM, 1)
        parts.append(jnp.where(lo if kv_half == 0 else jnp.logical_not(lo), blk, 0.0))
    return jnp.concatenate(parts, axis=0).astype(BF16)


def _unstack_group_b(o, rows, n_heads):
    lo = lax.broadcasted_iota(jnp.int32, (rows, LANES), 1) < HEAD_DIM
    per_kv = n_heads // 2
    cols = []
    for mblk in range(n_heads // 2):
        halves = []
        for u in range(2):
            i = 2 * mblk + u
            part = o[i * rows:(i + 1) * rows]
            if i // per_kv != u:
                part = pltpu.roll(part, HEAD_DIM, 1)
            halves.append(part)
        cols.append(jnp.where(lo, halves[0], halves[1]))
    return jnp.concatenate(cols, axis=1)


def _attn_a_prompt_kernel(q_ref, k_ref, v_ref, bias_ref, o_ref, *, rq, n_prev):
    qb = pl.program_id(2)
    qs = _stack_pair_a(q_ref[...] * SCALE)
    scores, values = [], []
    for j in range(n_prev + 1):
        blk = qb - n_prev + j
        start = pl.multiple_of(jnp.maximum(blk, 0) * rq, rq)
        kj = k_ref[pl.ds(start, rq), :].astype(BF16)
        values.append(v_ref[pl.ds(start, rq), :].astype(BF16))
        s = _nt_dot(qs, kj) + bias_ref[j]
        if j < n_prev:
            s = jnp.where(blk >= 0, s, NEG)
        scores.append(s)
    o = _softmax_pv(scores, values, row_parts=2)
    o_ref[...] = _unstack_pair_a(o, rq).astype(BF16)


def _attn_a_prompt(proj, bias, batch, seq, width_a, rq):
    n_pairs = width_a // LANES
    n_prev = BAND_CHUNKS_A * CHUNK // rq
    nqb = seq // rq
    kcol, vcol = width_a // LANES, 2 * width_a // LANES
    return pl.pallas_call(
        functools.partial(_attn_a_prompt_kernel, rq=rq, n_prev=n_prev),
        grid=(batch, n_pairs, nqb),
        in_specs=[
            pl.BlockSpec((rq, LANES), lambda b, p, i: (b * nqb + i, p)),
            pl.BlockSpec((seq, LANES), lambda b, p, i: (b, kcol + p)),
            pl.BlockSpec((seq, LANES), lambda b, p, i: (b, vcol + p)),
            pl.BlockSpec((None, n_prev + 1, 2 * rq, rq), lambda b, p, i: (p, 0, 0, 0)),
        ],
        out_specs=pl.BlockSpec((rq, LANES), lambda b, p, i: (b * nqb + i, p)),
        out_shape=jax.ShapeDtypeStruct((batch * seq, width_a), BF16),
        compiler_params=_cparams(("arbitrary", "arbitrary", "arbitrary")),
        name="attn_a_prompt",
    )(proj, proj, proj, bias)


def _attn_a_sample_kernel(q_ref, kn_ref, vn_ref, kc_ref, vc_ref, bias_ref, o_ref, *, n_cache):
    qs = _stack_pair_a(q_ref[...] * SCALE)
    bias = bias_ref[...]
    s_c = _nt_dot(qs, kc_ref[...].astype(BF16)) + bias[:, :n_cache]
    s_n = _nt_dot(qs, kn_ref[...].astype(BF16)) + bias[:, n_cache:]
    o = _softmax_pv([s_c, s_n], [vc_ref[...].astype(BF16), vn_ref[...].astype(BF16)])
    o_ref[...] = _unstack_pair_a(o, q_ref.shape[0]).astype(BF16)


def _attn_a_sample(proj, cache_k, cache_v, bias, layer, row0, dec_batch, dec_seq, width_a):
    n_pairs = width_a // LANES
    n_cache = cache_k.shape[2]
    rb0 = row0 // dec_seq
    kcol, vcol = width_a // LANES, 2 * width_a // LANES
    ck = cache_k.reshape(cache_k.shape[0], dec_batch, n_cache, width_a)
    cv = cache_v.reshape(cache_v.shape[0], dec_batch, n_cache, width_a)
    return pl.pallas_call(
        functools.partial(_attn_a_sample_kernel, n_cache=n_cache),
        grid=(dec_batch, n_pairs),
        in_specs=[
            pl.BlockSpec((dec_seq, LANES), lambda b, p: (rb0 + b, p)),
            pl.BlockSpec((dec_seq, LANES), lambda b, p: (rb0 + b, kcol + p)),
            pl.BlockSpec((dec_seq, LANES), lambda b, p: (rb0 + b, vcol + p)),
            pl.BlockSpec((None, None, n_cache, LANES), lambda b, p: (layer, b, 0, p)),
            pl.BlockSpec((None, None, n_cache, LANES), lambda b, p: (layer, b, 0, p)),
            pl.BlockSpec((None, 2 * dec_seq, n_cache + dec_seq), lambda b, p: (p, 0, 0)),
        ],
        out_specs=pl.BlockSpec((dec_seq, LANES), lambda b, p: (b, p)),
        out_shape=jax.ShapeDtypeStruct((dec_batch * dec_seq, width_a), BF16),
        compiler_params=_cparams(("arbitrary", "arbitrary")),
        name="attn_a_sample",
    )(proj, proj, proj, ck, cv, bias)


def _attn_b_prompt_kernel(q_ref, k_ref, v_ref, bias_ref, sink_ref, o_ref, *, rq, n_heads):
    qb = pl.program_id(2)
    qs = _stack_group_b(q_ref[...] * SCALE, n_heads)
    bias = bias_ref[...]
    prev = qb - 1
    p_start = pl.multiple_of(jnp.maximum(prev, 0) * rq, rq)
    o_start = pl.multiple_of(qb * rq, rq)
    s_p = _nt_dot(qs, k_ref[pl.ds(p_start, rq), :].astype(BF16)) + bias[:, :rq]
    s_p = jnp.where(prev >= 0, s_p, NEG)
    s_o = _nt_dot(qs, k_ref[pl.ds(o_start, rq), :].astype(BF16)) + bias[:, rq:]
    o = _softmax_pv([s_p, s_o],
                    [v_ref[pl.ds(p_start, rq), :].astype(BF16),
                     v_ref[pl.ds(o_start, rq), :].astype(BF16)],
                    sink_ref[...], mxu_sum=True)
    o_ref[...] = _unstack_group_b(o, rq, n_heads).astype(BF16)


def _attn_b_prompt(proj, bias, sink, batch, seq, qcol0, kcol0, vcol0, width_b, n_kv_pairs, rq):
    n_heads = width_b // HEAD_DIM // n_kv_pairs
    qw = n_heads * HEAD_DIM
    nqb = seq // rq
    qc, kc, vc = qcol0 // qw, kcol0 // LANES, vcol0 // LANES
    return pl.pallas_call(
        functools.partial(_attn_b_prompt_kernel, rq=rq, n_heads=n_heads),
        grid=(batch, n_kv_pairs, nqb),
        in_specs=[
            pl.BlockSpec((rq, qw), lambda b, p, i: (b * nqb + i, qc + p)),
            pl.BlockSpec((seq, LANES), lambda b, p, i: (b, kc + p)),
            pl.BlockSpec((seq, LANES), lambda b, p, i: (b, vc + p)),
            pl.BlockSpec((None, n_heads * rq, 2 * rq), lambda b, p, i: (p, 0, 0)),
            pl.BlockSpec((None, n_heads * rq, 1), lambda b, p, i: (p, 0, 0)),
        ],
        out_specs=pl.BlockSpec((rq, qw), lambda b, p, i: (b * nqb + i, p)),
        out_shape=jax.ShapeDtypeStruct((batch * seq, width_b), BF16),
        compiler_params=_cparams(("arbitrary", "arbitrary", "arbitrary")),
        name="attn_b_prompt",
    )(proj, proj, proj, bias, sink)


def _attn_b_sample_kernel(q_ref, kn_ref, vn_ref, kc_ref, vc_ref, bias_ref, sink_ref, o_ref, *,
                          n_cache, n_heads):
    qs = _stack_group_b(q_ref[...] * SCALE, n_heads)
    bias = bias_ref[...]
    s_c = _nt_dot(qs, kc_ref[...].astype(BF16)) + bias[:, :n_cache]
    s_n = _nt_dot(qs, kn_ref[...].astype(BF16)) + bias[:, n_cache:]
    o = _softmax_pv([s_c, s_n], [vc_ref[...].astype(BF16), vn_ref[...].astype(BF16)],
                    sink_ref[...])
    o_ref[...] = _unstack_group_b(o, q_ref.shape[0], n_heads).astype(BF16)


def _attn_b_sample(proj, cache_k, cache_v, bias, sink, layer, row0, dec_batch, dec_seq,
                   qcol0, kcol0, vcol0, width_b, n_kv_pairs):
    n_heads = width_b // HEAD_DIM // n_kv_pairs
    qw = n_heads * HEAD_DIM
    n_cache = cache_k.shape[2]
    kvw = cache_k.shape[3] * cache_k.shape[4]
    rb0 = row0 // dec_seq
    qc, kc, vc = qcol0 // qw, kcol0 // LANES, vcol0 // LANES
    ck = cache_k.reshape(cache_k.shape[0], dec_batch, n_cache, kvw)
    cv = cache_v.reshape(cache_v.shape[0], dec_batch, n_cache, kvw)
    return pl.pallas_call(
        functools.partial(_attn_b_sample_kernel, n_cache=n_cache, n_heads=n_heads),
        grid=(dec_batch, n_kv_pairs),
        in_specs=[
            pl.BlockSpec((dec_seq, qw), lambda b, p: (rb0 + b, qc + p)),
            pl.BlockSpec((dec_seq, LANES), lambda b, p: (rb0 + b, kc + p)),
            pl.BlockSpec((dec_seq, LANES), lambda b, p: (rb0 + b, vc + p)),
            pl.BlockSpec((None, None, n_cache, LANES), lambda b, p: (layer, b, 0, p)),
            pl.BlockSpec((None, None, n_cache, LANES), lambda b, p: (layer, b, 0, p)),
            pl.BlockSpec((None, n_heads * dec_seq, n_cache + dec_seq), lambda b, p: (p, 0, 0)),
            pl.BlockSpec((None, n_heads * dec_seq, 1), lambda b, p: (p, 0, 0)),
        ],
        out_specs=pl.BlockSpec((dec_seq, qw), lambda b, p: (b, p)),
        out_shape=jax.ShapeDtypeStruct((dec_batch * dec_seq, width_b), BF16),
        compiler_params=_cparams(("arbitrary", "arbitrary")),
        name="attn_b_sample",
    )(proj, proj, proj, ck, cv, bias, sink)


def _outproj_kernel(a_ref, w_ref, b_ref, x_ref, gate_ref, o_ref, wbf_ref):
    @pl.when(pl.program_id(1) == 0)
    def _():
        wbf_ref[...] = w_ref[...].astype(BF16)

    y = jnp.dot(a_ref[...], wbf_ref[...], preferred_element_type=F32) + b_ref[...]
    tm, tn = y.shape
    y = y.reshape(tm // CHUNK, CHUNK, tn) * gate_ref[...]
    o_ref[...] = x_ref[...] + y.reshape(tm, tn)


def _outproj(attn, w_out, b_out, x, modc, gate_col, layer):
    n, kdim = attn.shape
    d = w_out.shape[-1]
    tm = _pick(n, (512, 256, 128))
    tn = _pick(d, (512, 256, 128))
    cpt = tm // CHUNK
    gc = gate_col * (d // tn)
    return pl.pallas_call(
        _outproj_kernel,
        grid=(d // tn, n // tm),
        in_specs=[
            pl.BlockSpec((tm, kdim), lambda j, i: (i, 0)),
            pl.BlockSpec((None, kdim, tn), lambda j, i: (layer, 0, j)),
            pl.BlockSpec((None, 1, tn), lambda j, i: (layer, 0, j)),
            pl.BlockSpec((tm, tn), lambda j, i: (i, j)),
            pl.BlockSpec((cpt, 1, tn), lambda j, i: (i, 0, gc + j)),
        ],
        out_specs=pl.BlockSpec((tm, tn), lambda j, i: (i, j)),
        out_shape=jax.ShapeDtypeStruct((n, d), F32),
        scratch_shapes=[pltpu.VMEM((kdim, tn), BF16)],
        compiler_params=_cparams(("arbitrary", "arbitrary")),
        name="out_proj",
    )(attn, w_out, b_out, x, modc)


def _router_kernel(x_ref, g_ref, scale_ref, shift_ref, wr_ref, br_ref,
                   hp_ref, e_ref, gate_ref, rank_ref, cnt_ref, whi_ref, wlo_ref, *, n_experts):
    @pl.when(pl.program_id(0) == 0)
    def _():
        w = wr_ref[...]
        hi = w.astype(BF16)
        whi_ref[...] = hi
        wlo_ref[...] = (w - hi.astype(F32)).astype(BF16)

    h = _rms_mod(x_ref[...], g_ref[...], scale_ref[...], shift_ref[...])
    tm, d = h.shape
    hi = h.astype(BF16)
    hi32 = hi.astype(F32)
    lo = (h - hi32).astype(BF16)
    bits = lax.bitcast_convert_type(hi32, jnp.uint32)
    hp_ref[...] = bits[:, :d // 2] | (bits[:, d // 2:] >> 16)

    logits = (jnp.dot(hi, whi_ref[...], preferred_element_type=F32)
              + jnp.dot(lo, whi_ref[...], preferred_element_type=F32)
              + jnp.dot(hi, wlo_ref[...], preferred_element_type=F32)) + br_ref[...]

    lane_e = lax.broadcasted_iota(jnp.int32, (tm, n_experts), 1).astype(F32)
    lane_o = lax.broadcasted_iota(jnp.int32, (tm, LANES), 1)
    work = logits
    tops, idxs = [], []
    for _ in range(TOP_K):
        mx = jnp.max(work, axis=-1, keepdims=True)
        ix = jnp.min(jnp.where(work == mx, lane_e, float(n_experts)), axis=-1, keepdims=True)
        tops.append(mx)
        idxs.append(ix)
        work = jnp.where(lane_e == ix, -jnp.inf, work)
    exps = [jnp.exp(t - tops[0]) for t in tops]
    den = exps[0] + exps[1] + exps[2] + exps[3]
    e_out = jnp.zeros((tm, LANES), F32)
    g_out = jnp.zeros((tm, LANES), F32)
    for k in range(TOP_K):
        e_out = jnp.where(lane_o == k, idxs[k], e_out)
        g_out = jnp.where(lane_o == k, exps[k] / den, g_out)
    e_ref[...] = e_out.astype(jnp.int32)
    gate_ref[...] = g_out

    @pl.when(pl.program_id(0) == 0)
    def _():
        cnt_ref[...] = jnp.zeros_like(cnt_ref)

    lane_f = lane_o.astype(F32)
    sel = jnp.zeros((tm, LANES), F32)
    for k in range(TOP_K):
        sel = sel + (lane_f == idxs[k]).astype(F32)
    tri = (lax.broadcasted_iota(jnp.int32, (tm, tm), 0)
           > lax.broadcasted_iota(jnp.int32, (tm, tm), 1)).astype(BF16)
    before = jnp.dot(tri, sel.astype(BF16), preferred_element_type=F32) + cnt_ref[0:1, :]
    r_out = jnp.zeros((tm, LANES), F32)
    for k in range(TOP_K):
        rk = jnp.sum(jnp.where(lane_f == idxs[k], before, 0.0), axis=-1, keepdims=True)
        r_out = jnp.where(lane_o == k, rk, r_out)
    rank_ref[...] = r_out.astype(jnp.int32)
    cnt_ref[...] = cnt_ref[...] + jnp.sum(sel, axis=0, keepdims=True)


def _router(x, g_all, layer, modc, scale_col, shift_col, w_router, b_router):
    n, d = x.shape
    n_experts = w_router.shape[-1]
    tm = _pick(n, (256, 128, 64))
    cpt = tm // CHUNK
    depth = w_router.shape[0]
    return pl.pallas_call(
        functools.partial(_router_kernel, n_experts=n_experts),
        grid=(n // tm,),
        in_specs=[
            pl.BlockSpec((tm, d), lambda i: (i, 0)),
            pl.BlockSpec((None, 1, d), lambda i: (layer, 0, 0)),
            pl.BlockSpec((cpt, 1, d), lambda i: (i, 0, scale_col)),
            pl.BlockSpec((cpt, 1, d), lambda i: (i, 0, shift_col)),
            pl.BlockSpec((None, d, n_experts), lambda i: (layer, 0, 0)),
            pl.BlockSpec((None, 1, n_experts), lambda i: (layer, 0, 0)),
        ],
        out_specs=[
            pl.BlockSpec((tm, d // 2), lambda i: (i, 0)),
            pl.BlockSpec((tm, LANES), lambda i: (i, 0)),
            pl.BlockSpec((tm, LANES), lambda i: (i, 0)),
            pl.BlockSpec((tm, LANES), lambda i: (i, 0)),
            pl.BlockSpec((8, LANES), lambda i: (0, 0)),
        ],
        out_shape=[
            jax.ShapeDtypeStruct((n, d // 2), jnp.uint32),
            jax.ShapeDtypeStruct((n, LANES), jnp.int32),
            jax.ShapeDtypeStruct((n, LANES), F32),
            jax.ShapeDtypeStruct((n, LANES), jnp.int32),
            jax.ShapeDtypeStruct((8, LANES), F32),
        ],
        scratch_shapes=[pltpu.VMEM((d, n_experts), BF16), pltpu.VMEM((d, n_experts), BF16)],
        compiler_params=_cparams(("arbitrary",)),
        name="norm_router",
    )(x, g_all, modc, modc, w_router, b_router.reshape(depth, 1, n_experts))


DMA_ISSUE_UNROLL = 8
BULK_DMA_PRIORITY = 1


def _row_copy(src, dst, sem, src_row, dst_row):
    return pltpu.make_async_copy(src.at[pl.ds(src_row, 1)], dst.at[pl.ds(dst_row, 1)], sem)


def _tile_wait(src, dst, sem, tm):
    pltpu.make_async_copy(src.at[pl.ds(0, tm)], dst.at[pl.ds(0, tm)], sem).wait()


def _gather_kernel(tok_ref, h_hbm, o_hbm, buf_ref, in_sem, out_sem, *, tm):
    i = pl.program_id(0)
    n_steps = pl.num_programs(0)
    slot = i % 2

    def issue_tile(tile, s):
        base = tile * tm

        def issue(r, carry):
            _row_copy(h_hbm, buf_ref.at[s], in_sem.at[s], tok_ref[base + r], r).start()
            return carry

        lax.fori_loop(0, tm, issue, 0, unroll=DMA_ISSUE_UNROLL)

    def out_copy(tile, s):
        rows = pl.ds(pl.multiple_of(tile * tm, tm), tm)
        return pltpu.make_async_copy(buf_ref.at[s], o_hbm.at[rows], out_sem.at[s])

    @pl.when(i == 0)
    def _():
        issue_tile(0, 0)

    @pl.when(i >= 1)
    def _():
        out_copy(i - 1, 1 - slot).wait()

    @pl.when(i + 1 < n_steps)
    def _():
        issue_tile(i + 1, 1 - slot)

    _tile_wait(h_hbm, buf_ref.at[slot], in_sem.at[slot], tm)
    out_copy(i, slot).start(priority=BULK_DMA_PRIORITY)

    @pl.when(i == n_steps - 1)
    def _():
        out_copy(i, slot).wait()


def _gather_rows(row_tok, hp, tm):
    n_rows = row_tok.shape[0]
    grid_spec = pltpu.PrefetchScalarGridSpec(
        num_scalar_prefetch=1,
        grid=(n_rows // tm,),
        in_specs=[pl.BlockSpec(memory_space=pl.ANY)],
        out_specs=pl.BlockSpec(memory_space=pl.ANY),
        scratch_shapes=[pltpu.VMEM((2, tm, hp.shape[1]), hp.dtype),
                        pltpu.SemaphoreType.DMA((2,)), pltpu.SemaphoreType.DMA((2,))],
    )
    return pl.pallas_call(
        functools.partial(_gather_kernel, tm=tm),
        grid_spec=grid_spec,
        out_shape=jax.ShapeDtypeStruct((n_rows, hp.shape[1]), hp.dtype),
        compiler_params=_cparams(("arbitrary",)),
        name="moe_gather",
    )(row_tok, hp)


def _unpack_rows(words):
    first = lax.bitcast_convert_type(words & jnp.uint32(0xFFFF0000), F32).astype(BF16)
    second = lax.bitcast_convert_type(words << 16, F32).astype(BF16)
    return first, second


def _expert_tile_loop(ts_ref, tc_ref, nu_ref, src_hbm, dst_hbm, ibuf, obuf, in_sem, out_sem,
                      compute, *, tm, tn):
    j, e = pl.program_id(0), pl.program_id(1)
    first, count = ts_ref[e], tc_ref[e]
    cols = pl.ds(pl.multiple_of(j * tn, tn), tn)

    def rows(tile):
        return pl.ds(pl.multiple_of(tile * tm, tm), tm)

    def load(t, s):
        return pltpu.make_async_copy(src_hbm.at[rows(first + t)], ibuf.at[s], in_sem.at[s])

    def store(tile, s):
        return pltpu.make_async_copy(obuf.at[s], dst_hbm.at[rows(tile), cols], out_sem.at[s])

    @pl.when(count > 0)
    def _():
        load(0, 0).start(priority=BULK_DMA_PRIORITY)

    def body(t, carry):
        s = t % 2
        load(t, s).wait()

        @pl.when(t + 1 < count)
        def _():
            load(t + 1, 1 - s).start(priority=BULK_DMA_PRIORITY)

        @pl.when(t >= 2)
        def _():
            store(first + t - 2, s).wait()

        obuf[s] = compute(ibuf[s])
        store(first + t, s).start(priority=BULK_DMA_PRIORITY)
        return carry

    lax.fori_loop(0, count, body, 0)

    @pl.when(count >= 2)
    def _():
        store(first + count - 2, count % 2).wait()

    @pl.when(count >= 1)
    def _():
        store(first + count - 1, (count - 1) % 2).wait()

    @pl.when(e == pl.num_programs(1) - 1)
    def _():
        obuf[0] = jnp.zeros(obuf.shape[1:], obuf.dtype)

        def fill(tile, carry):
            cp = store(tile, 0)
            cp.start()
            cp.wait()
            return carry

        lax.fori_loop(nu_ref[0], dst_hbm.shape[0] // tm, fill, 0)


def _up_kernel(ts_ref, tc_ref, nu_ref, x_hbm, wg_ref, wl_ref, bg_ref, bl_ref, o_hbm,
               xbuf, obuf, wgb_ref, wlb_ref, in_sem, out_sem, *, tm, tn):
    wgb_ref[...] = wg_ref[...].astype(BF16)
    wlb_ref[...] = wl_ref[...].astype(BF16)

    def compute(words):
        x0, x1 = _unpack_rows(words)
        half = x0.shape[1]
        glu = (jnp.dot(x0, wgb_ref[:half, :], preferred_element_type=F32)
               + jnp.dot(x1, wgb_ref[half:, :], preferred_element_type=F32)) + bg_ref[...]
        lin = (jnp.dot(x0, wlb_ref[:half, :], preferred_element_type=F32)
               + jnp.dot(x1, wlb_ref[half:, :], preferred_element_type=F32)) + bl_ref[...]
        glu = jnp.minimum(glu, SWIGLU_LIMIT)
        lin = jnp.clip(lin, -SWIGLU_LIMIT, SWIGLU_LIMIT)
        act = glu * (1.0 / (1.0 + jnp.exp(-SWIGLU_ALPHA * glu))) * (lin + 1.0)
        return act.astype(BF16)

    _expert_tile_loop(ts_ref, tc_ref, nu_ref, x_hbm, o_hbm, xbuf, obuf, in_sem, out_sem, compute,
                      tm=tm, tn=tn)


def _expert_up(tile_first, tile_count, n_used, xs, w_up, b_up, layer, tm):
    depth, n_experts, d, two_ff = w_up.shape
    n_rows, half = xs.shape
    d_ff = two_ff // 2
    tn = _pick(d_ff, (512, 256, 128))
    nj = d_ff // tn
    b4 = b_up.reshape(depth, n_experts, 1, two_ff)
    grid_spec = pltpu.PrefetchScalarGridSpec(
        num_scalar_prefetch=3,
        grid=(nj, n_experts),
        in_specs=[
            pl.BlockSpec(memory_space=pl.ANY),
            pl.BlockSpec((None, None, d, tn), lambda j, e, *_: (layer, e, 0, j)),
            pl.BlockSpec((None, None, d, tn), lambda j, e, *_: (layer, e, 0, nj + j)),
            pl.BlockSpec((None, None, 1, tn), lambda j, e, *_: (layer, e, 0, j)),
            pl.BlockSpec((None, None, 1, tn), lambda j, e, *_: (layer, e, 0, nj + j)),
        ],
        out_specs=pl.BlockSpec(memory_space=pl.ANY),
        scratch_shapes=[pltpu.VMEM((2, tm, half), xs.dtype), pltpu.VMEM((2, tm, tn), BF16),
                        pltpu.VMEM((d, tn), BF16), pltpu.VMEM((d, tn), BF16),
                        pltpu.SemaphoreType.DMA((2,)), pltpu.SemaphoreType.DMA((2,))],
    )
    return pl.pallas_call(
        functools.partial(_up_kernel, tm=tm, tn=tn),
        grid_spec=grid_spec,
        out_shape=jax.ShapeDtypeStruct((n_rows, d_ff), BF16),
        compiler_params=_cparams(("arbitrary", "arbitrary")),
        name="expert_up",
    )(tile_first, tile_count, n_used, xs, w_up, w_up, b4, b4)


def _down_kernel(ts_ref, tc_ref, nu_ref, a_hbm, w_ref, b_ref, o_hbm,
                 abuf, obuf, wb_ref, in_sem, out_sem, *, tm, tn):
    wb_ref[...] = w_ref[...].astype(BF16)

    def compute(a):
        return jnp.dot(a, wb_ref[...], preferred_element_type=F32) + b_ref[...]

    _expert_tile_loop(ts_ref, tc_ref, nu_ref, a_hbm, o_hbm, abuf, obuf, in_sem, out_sem, compute,
                      tm=tm, tn=tn)


def _expert_down(tile_first, tile_count, n_used, act, w_down, b_down, layer, tm):
    n_rows, d_ff = act.shape
    depth, n_experts, _, d = w_down.shape
    tn = _pick(d, (2048, 1024, 512, 256, 128))
    b4 = b_down.reshape(depth, n_experts, 1, d)
    grid_spec = pltpu.PrefetchScalarGridSpec(
        num_scalar_prefetch=3,
        grid=(d // tn, n_experts),
        in_specs=[
            pl.BlockSpec(memory_space=pl.ANY),
            pl.BlockSpec((None, None, d_ff, tn), lambda j, e, *_: (layer, e, 0, j)),
            pl.BlockSpec((None, None, 1, tn), lambda j, e, *_: (layer, e, 0, j)),
        ],
        out_specs=pl.BlockSpec(memory_space=pl.ANY),
        scratch_shapes=[pltpu.VMEM((2, tm, d_ff), act.dtype), pltpu.VMEM((2, tm, tn), F32),
                        pltpu.VMEM((d_ff, tn), BF16),
                        pltpu.SemaphoreType.DMA((2,)), pltpu.SemaphoreType.DMA((2,))],
    )
    return pl.pallas_call(
        functools.partial(_down_kernel, tm=tm, tn=tn),
        grid_spec=grid_spec,
        out_shape=jax.ShapeDtypeStruct((n_rows, d), F32),
        compiler_params=_cparams(("arbitrary", "arbitrary")),
        name="expert_down",
    )(tile_first, tile_count, n_used, act, w_down, b4)


def _combine_kernel(pos_ref, gate_ref, x_ref, g2_ref, y_hbm, o_ref, buf_ref, sem, *, tm):
    i = pl.program_id(0)
    n_steps = pl.num_programs(0)

    def issue_tile(tile, slot):
        base = tile * (tm * TOP_K)

        def issue(r, carry):
            for k in range(TOP_K):
                _row_copy(y_hbm, buf_ref.at[slot, k], sem.at[slot],
                          pos_ref[base + r * TOP_K + k], r).start(priority=BULK_DMA_PRIORITY)
            return carry

        lax.fori_loop(0, tm, issue, 0, unroll=DMA_ISSUE_UNROLL // TOP_K)

    slot = i % 2

    @pl.when(i == 0)
    def _():
        issue_tile(0, 0)

    @pl.when(i + 1 < n_steps)
    def _():
        issue_tile(i + 1, 1 - slot)

    for k in range(TOP_K):
        _tile_wait(y_hbm, buf_ref.at[slot, k], sem.at[slot], tm)

    gate = gate_ref[...]
    acc = gate[:, 0:1] * buf_ref[slot, 0]
    for k in range(1, TOP_K):
        acc = acc + gate[:, k:k + 1] * buf_ref[slot, k]
    d = acc.shape[1]
    acc = acc.reshape(tm // CHUNK, CHUNK, d) * g2_ref[...]
    o_ref[...] = x_ref[...] + acc.reshape(tm, d)


def _combine(pos, gate, x, modc, gate_col, yb):
    n, d = x.shape
    tm = _pick(n, (128, 64))
    cpt = tm // CHUNK
    grid_spec = pltpu.PrefetchScalarGridSpec(
        num_scalar_prefetch=1,
        grid=(n // tm,),
        in_specs=[
            pl.BlockSpec((tm, LANES), lambda i, p: (i, 0)),
            pl.BlockSpec((tm, d), lambda i, p: (i, 0)),
            pl.BlockSpec((cpt, 1, d), lambda i, p: (i, 0, gate_col)),
            pl.BlockSpec(memory_space=pl.ANY),
        ],
        out_specs=pl.BlockSpec((tm, d), lambda i, p: (i, 0)),
        scratch_shapes=[pltpu.VMEM((2, TOP_K, tm, d), F32), pltpu.SemaphoreType.DMA((2,))],
    )
    return pl.pallas_call(
        functools.partial(_combine_kernel, tm=tm),
        grid_spec=grid_spec,
        out_shape=jax.ShapeDtypeStruct((n, d), F32),
        compiler_params=_cparams(("arbitrary",)),
        name="moe_combine",
    )(pos, gate, x, modc, yb)


def _dispatch_plan(top_e, rank, counts, n_experts, tm):
    n_tok = top_e.shape[0]
    n_assign = n_tok * TOP_K
    flat_e = top_e.reshape(n_assign)
    rank = rank.reshape(n_assign)
    padded = (counts + tm - 1) // tm * tm
    p_end = jnp.cumsum(padded)
    p_start = p_end - padded
    dest = (p_start[flat_e] + rank).astype(jnp.int32)
    n_tiles = n_assign // tm + n_experts
    filler = jnp.arange(n_tiles * tm, dtype=jnp.int32) % n_tok
    row_tok = filler.at[dest].set(jnp.arange(n_assign, dtype=jnp.int32) // TOP_K)
    tile_first = (p_start // tm).astype(jnp.int32)
    tile_count = (padded // tm).astype(jnp.int32)
    n_used = (p_end[-1] // tm).astype(jnp.int32).reshape(1)
    return dest, row_tok, tile_first, tile_count, n_used


def _bias_a(rel_bias, rq, n_keys, masked):
    n_heads = rel_bias.shape[0]
    qi = jnp.arange(rq)[:, None]
    ki = jnp.arange(n_keys)[None, :]
    off = n_keys - rq
    n_diag = n_keys + rq - 1
    rel = jnp.clip(jnp.arange(n_diag) - (rq - 1) - off, -REL_CLIP, REL_CLIP) + REL_CLIP
    ext = jnp.pad(jnp.take(rel_bias, rel, axis=1), ((0, 0), (0, 1)))
    skew = jnp.tile(ext, (1, rq))[:, :rq * n_diag].reshape(n_heads, rq, n_diag)
    bias = skew[:, :, rq - 1:rq - 1 + n_keys]
    if masked:
        qc = (qi + off) // CHUNK
        kc = ki // CHUNK
        ok = (kc <= qc) & (kc >= qc - BAND_CHUNKS_A)
        bias = jnp.where(ok[None], bias, NEG)
    return bias.reshape(n_heads // 2, 2 * rq, n_keys)


def _bias_b(n_heads_total, n_kv_pairs, rq, n_keys, masked):
    slopes = 2.0 ** (-8.0 * jnp.arange(1, n_heads_total + 1, dtype=F32) / n_heads_total)
    qi = jnp.arange(rq)[:, None]
    ki = jnp.arange(n_keys)[None, :]
    off = n_keys - rq
    dist = jnp.abs(qi - (ki - off)).astype(F32)
    bias = -slopes[:, None, None] * dist[None]
    if masked:
        qc = (qi + off) // CHUNK
        kc = ki // CHUNK
        ok = (kc <= qc) & (kc >= qc - BAND_CHUNKS_B)
        bias = jnp.where(ok[None], bias, NEG)
    return bias.reshape(n_kv_pairs, (n_heads_total // n_kv_pairs) * rq, n_keys)


def kernel(x_prompt, x_sample, c_prompt, c_sample, cache_k_a, cache_v_a, cache_k_b, cache_v_b,
           w_ada, b_ada, g_attn, g_ffn, w_in, b_in, g_q_a, g_k_a, g_q_b, g_k_b, rel_bias_a,
           sinks_b, w_out, b_out, w_router, b_router, w_up, b_up, w_down, b_down):
    batch, seq, d = x_prompt.shape
    dec_batch, dec_seq, _ = x_sample.shape
    depth = w_in.shape[0]
    heads_a = rel_bias_a.shape[1]
    heads_b = sinks_b.shape[1]
    kv_heads_b = cache_k_b.shape[3]
    n_experts = w_router.shape[-1]
    width_a, width_b, kv_width_b = heads_a * HEAD_DIM, heads_b * HEAD_DIM, kv_heads_b * HEAD_DIM
    in_width = w_in.shape[-1]
    n_cache_a, n_cache_b = cache_k_a.shape[2], cache_k_b.shape[2]
    band_a, band_b = BAND_CHUNKS_A * CHUNK, BAND_CHUNKS_B * CHUNK
    assert dec_seq == CHUNK and PAST_LEN % CHUNK == 0 and seq % CHUNK == 0
    assert n_cache_a == band_a and n_cache_b == band_b
    assert in_width == 3 * width_a + width_b + 2 * kv_width_b and width_a + width_b == d
    assert kv_heads_b % 2 == 0 and heads_a % 2 == 0

    n_p, n_s = batch * seq, dec_batch * dec_seq
    n_tok = n_p + n_s
    x = jnp.concatenate([x_prompt.reshape(n_p, d), x_sample.reshape(n_s, d)], axis=0)

    n_seq = batch + dec_batch
    c_all = jnp.concatenate([c_prompt, c_sample], axis=0)
    c_pad = jnp.pad(c_all, ((0, -n_seq % 8), (0, 0)))
    mod_all = _ada(c_pad, w_ada, b_ada)
    shift1_c, scale1_c, gate1_c, shift2_c, scale2_c, gate2_c = range(N_MOD)

    seg = (width_a, 2 * width_a, 3 * width_a, 3 * width_a + width_b,
           3 * width_a + width_b + kv_width_b, in_width)
    tn_in = _pick(math.gcd(*seg), (512, 256, 128))
    col = jnp.arange(in_width)
    is_v = ((col >= seg[1]) & (col < seg[2])) | (col >= seg[4])
    flags = jnp.logical_not(is_v[::tn_in]).astype(jnp.int32)

    def gains(l):
        ones_a, ones_kv = jnp.ones((width_a,), F32), jnp.ones((kv_width_b,), F32)
        return jnp.concatenate([jnp.tile(g_q_a[l], heads_a), jnp.tile(g_k_a[l], heads_a), ones_a,
                                jnp.tile(g_q_b[l], heads_b), jnp.tile(g_k_b[l], kv_heads_b),
                                ones_kv])

    gain_all = jnp.stack([gains(l) for l in range(depth)]).reshape(depth, 1, in_width)
    b_in3 = b_in.reshape(depth, 1, in_width)
    b_out3 = b_out.reshape(depth, 1, d)
    g_attn3 = g_attn.reshape(depth, 1, d)
    g_ffn3 = g_ffn.reshape(depth, 1, d)

    rq_a = 4 * CHUNK
    rq_b = 2 * CHUNK
    n_kv_pairs = kv_heads_b // 2
    heads_per_pair = heads_b // n_kv_pairs
    bias_bp = _bias_b(heads_b, n_kv_pairs, rq_b, 2 * rq_b, True)
    bias_bs = _bias_b(heads_b, n_kv_pairs, dec_seq, n_cache_b + dec_seq, False)

    tm_moe = 256
    new_ka_p, new_va_p, new_kb_p, new_vb_p = [], [], [], []
    new_ka_s, new_va_s, new_kb_s, new_vb_s = [], [], [], []
    for l in range(depth):
        mod_p = jnp.broadcast_to(mod_all[l, :batch, None, :], (batch, seq // CHUNK, N_MOD * d))
        modc = jnp.concatenate([mod_p.reshape(n_p // CHUNK, N_MOD * d), mod_all[l, batch:n_seq]],
                               axis=0).reshape(n_tok // CHUNK, 1, N_MOD * d)
        h = _norm(x, g_attn3, l, modc, scale1_c, shift1_c)
        proj = _inproj(h, w_in, b_in3, gain_all, flags, l, tn_in)

        bias_ap = _bias_a(rel_bias_a[l], rq_a, band_a + rq_a, True)
        bias_ap = bias_ap.reshape(heads_a // 2, 2 * rq_a, -1, rq_a).transpose(0, 2, 1, 3)
        bias_as = _bias_a(rel_bias_a[l], dec_seq, n_cache_a + dec_seq, False)
        sink_p = jnp.repeat(sinks_b[l], rq_b).reshape(n_kv_pairs, heads_per_pair * rq_b, 1)
        sink_s = jnp.repeat(sinks_b[l], dec_seq).reshape(n_kv_pairs, heads_per_pair * dec_seq, 1)

        oa_p = _attn_a_prompt(proj, bias_ap, batch, seq, width_a, rq_a)
        oa_s = _attn_a_sample(proj, cache_k_a, cache_v_a, bias_as, l, n_p, dec_batch, dec_seq,
                              width_a)
        ob_p = _attn_b_prompt(proj, bias_bp, sink_p, batch, seq, seg[2], seg[3], seg[4],
                              width_b, n_kv_pairs, rq_b)
        ob_s = _attn_b_sample(proj, cache_k_b, cache_v_b, bias_bs, sink_s, l, n_p, dec_batch,
                              dec_seq, seg[2], seg[3], seg[4], width_b, n_kv_pairs)
        attn = jnp.concatenate([jnp.concatenate([oa_p, ob_p], axis=1),
                                jnp.concatenate([oa_s, ob_s], axis=1)], axis=0)
        x = _outproj(attn, w_out, b_out3, x, modc, gate1_c, l)

        hp, e_pad, gate_pad, rank_pad, cnt = _router(x, g_ffn3, l, modc, scale2_c, shift2_c,
                                                     w_router, b_router)
        dest, row_tok, tile_first, tile_count, n_used = _dispatch_plan(
            e_pad[:, :TOP_K], rank_pad[:, :TOP_K], cnt[0, :n_experts].astype(jnp.int32),
            n_experts, tm_moe)
        xs = _gather_rows(row_tok, hp, tm_moe)
        act = _expert_up(tile_first, tile_count, n_used, xs, w_up, b_up, l, tm_moe)
        yb = _expert_down(tile_first, tile_count, n_used, act, w_down, b_down, l, tm_moe)
        x = _combine(dest, gate_pad, x, modc, gate2_c, yb)

        ka = proj[:, seg[0]:seg[1]]
        va = proj[:, seg[1]:seg[2]]
        kb = proj[:, seg[3]:seg[4]]
        vb = proj[:, seg[4]:seg[5]]
        for dst, src, nh, keep in ((new_ka_p, ka, heads_a, band_a), (new_va_p, va, heads_a, band_a),
                                   (new_kb_p, kb, kv_heads_b, band_b),
                                   (new_vb_p, vb, kv_heads_b, band_b)):
            last = jnp.stack([src[(b + 1) * seq - keep:(b + 1) * seq] for b in range(batch)])
            dst.append(last.reshape(batch, keep, nh, HEAD_DIM))
        for dst, src, nh in ((new_ka_s, ka, heads_a), (new_va_s, va, heads_a),
                             (new_kb_s, kb, kv_heads_b), (new_vb_s, vb, kv_heads_b)):
            dst.append(src[n_p:].reshape(dec_batch, dec_seq, nh, HEAD_DIM))

    return (x[:n_p].reshape(batch, seq, d), x[n_p:].reshape(dec_batch, dec_seq, d),
            jnp.stack(new_ka_p), jnp.stack(new_va_p), jnp.stack(new_kb_p), jnp.stack(new_vb_p),
            jnp.stack(new_ka_s), jnp.stack(new_va_s), jnp.stack(new_kb_s), jnp.stack(new_vb_s))
```
